```python
import math
import jax, jax.numpy as jnp
from jax import lax
import numpy as np

D_MODEL = 2048
BATCH = 4
SEQ = 4096
DEPTH = 2
DEC_BATCH = 32
DEC_SEQ = 64
PAST_LEN = 2048

CHUNK = 64
N_PAST_CHUNKS = 8
BAND_PAST = N_PAST_CHUNKS * CHUNK
N_A_LAYERS = DEPTH // 2
N_B_LAYERS = DEPTH - N_A_LAYERS
SSM_GROUP = 16
SSM_GROUPS = D_MODEL // SSM_GROUP
SSM_STATE = 64
N_HEADS = 16
HEAD_DIM = D_MODEL // N_HEADS
REL_CLIP = 256
D_FF = (D_MODEL * 11) // 4
PLE_DIM = 256
RMS_EPS = 1e-6
MASK_VALUE = -1e30

kernel_name = 'streaming_s5_chunkband_yoco'


def rms_norm(x, g):
    xf = x.astype(jnp.float32)
    y = xf * lax.rsqrt(jnp.mean(xf * xf, axis=-1, keepdims=True) + RMS_EPS) * g.astype(jnp.float32)
    return y.astype(x.dtype)


def swiglu(x, w_gate, w_up, w_down):
    return (jax.nn.silu(x @ w_gate) * (x @ w_up)) @ w_down


def per_layer_embedding(h, p_i, norm_g, w_gate, w_proj):
    gate = jax.nn.sigmoid(rms_norm(h, norm_g) @ w_gate)
    return (p_i.astype(h.dtype) @ w_proj) * gate


def _ssm_combine(left, right):
    a_l, b_l = left
    a_r, b_r = right
    return a_l * a_r, a_r * b_l + b_r


def s5_mixer(u, h0, lam_re, lam_im, log_dt, b_re, b_im, c_re, c_im, d_skip, w_glu_a, w_glu_b):
    bsz, seq, _ = u.shape
    f32 = jnp.float32
    lam = lax.complex(lam_re.astype(f32), lam_im.astype(f32))
    dt = jnp.exp(log_dt.astype(f32))[:, None]
    lam_bar = jnp.exp(lam * dt)
    b_bar = ((lam_bar - 1.0) / lam)[:, :, None] * lax.complex(b_re.astype(f32), b_im.astype(f32))
    c_mat = lax.complex(c_re.astype(f32), c_im.astype(f32))
    blk = CHUNK if seq % CHUNK == 0 else seq
    n_blk = seq // blk
    u_blocks = jnp.moveaxis(u.astype(f32).reshape(bsz, n_blk, blk, SSM_GROUPS, SSM_GROUP), 1, 0)

    def step(h, u_blk):
        bu = jnp.einsum('gpc,btgc->btgp', b_bar, u_blk)
        bu = bu.at[:, 0].add(lam_bar * h)
        a = jnp.broadcast_to(lam_bar, bu.shape)
        _, hs = lax.associative_scan(_ssm_combine, (a, bu), axis=1)
        y_blk = jnp.einsum('gcp,btgp->btgc', c_mat, hs).real
        return hs[:, -1], y_blk

    h_last, y_blocks = lax.scan(step, h0, u_blocks)
    y = jnp.moveaxis(y_blocks, 0, 1).reshape(bsz, seq, D_MODEL) + d_skip.astype(f32) * u.astype(f32)
    z = jax.nn.gelu(y).astype(u.dtype)
    out = (z @ w_glu_a) * jax.nn.sigmoid(z @ w_glu_b)
    return out, h_last


def shared_kv(h, kv_norm_g, w_k, w_v, k_norm_g):
    bsz, seq, _ = h.shape
    hn = rms_norm(h, kv_norm_g)
    k = rms_norm((hn @ w_k).reshape(bsz, seq, N_HEADS, HEAD_DIM), k_norm_g)
    v = (hn @ w_v).reshape(bsz, seq, N_HEADS, HEAD_DIM)
    return k, v


def rel_index(dist):
    return jnp.clip(dist, -REL_CLIP, REL_CLIP) + REL_CLIP


def attend(q, k, v, bias, valid):
    s = jnp.einsum('bqhd,bkhd->bhqk', q, k).astype(jnp.float32) + bias
    if valid is not None:
        s = jnp.where(valid, s, MASK_VALUE)
    prob = jax.nn.softmax(s, axis=-1).astype(v.dtype)
    return jnp.einsum('bhqk,bkhd->bqhd', prob, v)


def prompt_band_attention(q, k, v, bias_tab):
    bsz, seq, n_h, d_h = q.shape
    n_chunks = seq // CHUNK
    band = BAND_PAST + CHUNK
    pad = ((0, 0), (BAND_PAST, 0), (0, 0), (0, 0))
    k_pad = jnp.pad(k, pad)
    v_pad = jnp.pad(v, pad)
    t = jnp.arange(CHUNK)
    j = jnp.arange(band)
    bias = bias_tab.astype(jnp.float32)[:, rel_index(BAND_PAST + t[:, None] - j[None, :])]
    q_chunks = jnp.moveaxis(q.reshape(bsz, n_chunks, CHUNK, n_h, d_h), 1, 0)

    def one_chunk(args):
        q_blk, c = args
        start = c * CHUNK
        k_blk = lax.dynamic_slice_in_dim(k_pad, start, band, axis=1)
        v_blk = lax.dynamic_slice_in_dim(v_pad, start, band, axis=1)
        valid = (start - BAND_PAST + j) >= 0
        return attend(q_blk, k_blk, v_blk, bias, valid)

    o = lax.map(one_chunk, (q_chunks, jnp.arange(n_chunks)))
    return jnp.moveaxis(o, 0, 1).reshape(bsz, seq, n_h, d_h)


def sample_band_attention(q, k_new, v_new, k_cache, v_cache, bias_tab):
    n_cache = k_cache.shape[1]
    seq = q.shape[1]
    k_all = jnp.concatenate([k_cache.astype(k_new.dtype), k_new], axis=1)
    v_all = jnp.concatenate([v_cache.astype(v_new.dtype), v_new], axis=1)
    q_pos = n_cache + jnp.arange(seq)
    k_pos = jnp.arange(n_cache + seq)
    bias = bias_tab.astype(jnp.float32)[:, rel_index(q_pos[:, None] - k_pos[None, :])]
    return attend(q, k_all, v_all, bias, None)


def chunk_band_mixer(hn, k_sh, v_sh, k_cache, v_cache, w_q, q_norm_g, bias_tab, w_o):
    bsz, seq, _ = hn.shape
    q = rms_norm((hn @ w_q).reshape(bsz, seq, N_HEADS, HEAD_DIM), q_norm_g) * (HEAD_DIM ** -0.5)
    if k_cache is None:
        o = prompt_band_attention(q, k_sh, v_sh, bias_tab)
    else:
        o = sample_band_attention(q, k_sh, v_sh, k_cache, v_cache, bias_tab)
    return o.reshape(bsz, seq, N_HEADS * HEAD_DIM) @ w_o


def run_trunk(x, p, ssm_h0, k_cache, v_cache, weights):
    (ffn1_norm_g, ffn1_w_gate, ffn1_w_up, ffn1_w_down,
     ffn2_norm_g, ffn2_w_gate, ffn2_w_up, ffn2_w_down,
     mix_norm_g, ssm_lam_re, ssm_lam_im, ssm_log_dt, ssm_b_re, ssm_b_im,
     ssm_c_re, ssm_c_im, ssm_d, ssm_w_glu_a, ssm_w_glu_b,
     kv_norm_g, w_k, w_v, k_norm_g, w_q, q_norm_g, rel_bias, w_o,
     ple_norm_g, ple_w_gate, ple_w_proj) = weights
    h = x
    ssm_final = []
    k_sh = None
    v_sh = None
    for i in range(DEPTH):
        if i == N_A_LAYERS:
            k_sh, v_sh = shared_kv(h, kv_norm_g, w_k, w_v, k_norm_g)
        h = h + 0.5 * swiglu(rms_norm(h, ffn1_norm_g[i]), ffn1_w_gate[i], ffn1_w_up[i], ffn1_w_down[i])
        hn = rms_norm(h, mix_norm_g[i])
        if i < N_A_LAYERS:
            mix, h_last = s5_mixer(hn, ssm_h0[i], ssm_lam_re[i], ssm_lam_im[i], ssm_log_dt[i],
                                   ssm_b_re[i], ssm_b_im[i], ssm_c_re[i], ssm_c_im[i], ssm_d[i],
                                   ssm_w_glu_a[i], ssm_w_glu_b[i])
            ssm_final.append(h_last)
        else:
            jb = i - N_A_LAYERS
            mix = chunk_band_mixer(hn, k_sh, v_sh, k_cache, v_cache, w_q[jb], q_norm_g[jb], rel_bias[jb], w_o[jb])
        h = h + mix
        h = h + 0.5 * swiglu(rms_norm(h, ffn2_norm_g[i]), ffn2_w_gate[i], ffn2_w_up[i], ffn2_w_down[i])
        h = h + per_layer_embedding(h, p[i], ple_norm_g[i], ple_w_gate[i], ple_w_proj[i])
    if k_cache is None:
        rows = min(BAND_PAST, x.shape[1])
        k_rows = k_sh[:, x.shape[1] - rows:]
        v_rows = v_sh[:, x.shape[1] - rows:]
    else:
        k_rows = k_sh
        v_rows = v_sh
    return h, jnp.stack(ssm_final, axis=0), k_rows, v_rows


def setup_inputs(seed: int = 0) -> dict:
    key = jax.random.key(seed)
    ks = iter(jax.random.split(key, 48))

    def nrm(shape, scale=1.0):
        return jax.random.normal(next(ks), shape, jnp.float32) * scale

    def gain(shape):
        return 1.0 + nrm(shape, 0.02)

    cache_rows = min(BAND_PAST, PAST_LEN)
    hd = N_HEADS * HEAD_DIM
    n_idx = jnp.arange(SSM_STATE, dtype=jnp.float32)
    ssm_shape = (N_A_LAYERS, SSM_GROUPS, SSM_STATE)
    return {
        'x_prompt': nrm((BATCH, SEQ, D_MODEL)),
        'x_sample': nrm((DEC_BATCH, DEC_SEQ, D_MODEL)),
        'state_ssm_re': nrm((N_A_LAYERS, DEC_BATCH, SSM_GROUPS, SSM_STATE), 0.3),
        'state_ssm_im': nrm((N_A_LAYERS, DEC_BATCH, SSM_GROUPS, SSM_STATE), 0.3),
        'cache_k': nrm((DEC_BATCH, cache_rows, N_HEADS, HEAD_DIM)),
        'cache_v': nrm((DEC_BATCH, cache_rows, N_HEADS, HEAD_DIM)),
        'p_prompt': nrm((DEPTH, BATCH, SEQ, PLE_DIM)),
        'p_sample': nrm((DEPTH, DEC_BATCH, DEC_SEQ, PLE_DIM)),
        'ffn1_norm_g': gain((DEPTH, D_MODEL)),
        'ffn1_w_gate': nrm((DEPTH, D_MODEL, D_FF), D_MODEL ** -0.5),
        'ffn1_w_up': nrm((DEPTH, D_MODEL, D_FF), D_MODEL ** -0.5),
        'ffn1_w_down': nrm((DEPTH, D_FF, D_MODEL), D_FF ** -0.5),
        'ffn2_norm_g': gain((DEPTH, D_MODEL)),
        'ffn2_w_gate': nrm((DEPTH, D_MODEL, D_FF), D_MODEL ** -0.5),
        'ffn2_w_up': nrm((DEPTH, D_MODEL, D_FF), D_MODEL ** -0.5),
        'ffn2_w_down': nrm((DEPTH, D_FF, D_MODEL), D_FF ** -0.5),
        'mix_norm_g': gain((DEPTH, D_MODEL)),
        'ssm_lam_re': -0.5 + nrm(ssm_shape, 0.01),
        'ssm_lam_im': math.pi * n_idx + nrm(ssm_shape, 0.01),
        'ssm_log_dt': jax.random.uniform(next(ks), (N_A_LAYERS, SSM_GROUPS), jnp.float32,
                                         math.log(1e-3), math.log(1e-1)),
        'ssm_b_re': nrm((N_A_LAYERS, SSM_GROUPS, SSM_STATE, SSM_GROUP), (2 * SSM_GROUP) ** -0.5),
        'ssm_b_im': nrm((N_A_LAYERS, SSM_GROUPS, SSM_STATE, SSM_GROUP), (2 * SSM_GROUP) ** -0.5),
        'ssm_c_re': nrm((N_A_LAYERS, SSM_GROUPS, SSM_GROUP, SSM_STATE), SSM_STATE ** -0.5),
        'ssm_c_im': nrm((N_A_LAYERS, SSM_GROUPS, SSM_GROUP, SSM_STATE), SSM_STATE ** -0.5),
        'ssm_d': nrm((N_A_LAYERS, D_MODEL)),
        'ssm_w_glu_a': nrm((N_A_LAYERS, D_MODEL, D_MODEL), D_MODEL ** -0.5),
        'ssm_w_glu_b': nrm((N_A_LAYERS, D_MODEL, D_MODEL), D_MODEL ** -0.5),
        'kv_norm_g': gain((D_MODEL,)),
        'w_k': nrm((D_MODEL, hd), D_MODEL ** -0.5),
        'w_v': nrm((D_MODEL, hd), D_MODEL ** -0.5),
        'k_norm_g': gain((HEAD_DIM,)),
        'w_q': nrm((N_B_LAYERS, D_MODEL, hd), D_MODEL ** -0.5),
        'q_norm_g': gain((N_B_LAYERS, HEAD_DIM)),
        'rel_bias': nrm((N_B_LAYERS, N_HEADS, 2 * REL_CLIP + 1), 0.1),
        'w_o': nrm((N_B_LAYERS, hd, D_MODEL), hd ** -0.5),
        'ple_norm_g': gain((DEPTH, D_MODEL)),
        'ple_w_gate': nrm((DEPTH, D_MODEL, D_MODEL), D_MODEL ** -0.5),
        'ple_w_proj': nrm((DEPTH, PLE_DIM, D_MODEL), PLE_DIM ** -0.5),
    }


def reference(x_prompt, x_sample, state_ssm_re, state_ssm_im, cache_k, cache_v, p_prompt, p_sample,
              ffn1_norm_g, ffn1_w_gate, ffn1_w_up, ffn1_w_down,
              ffn2_norm_g, ffn2_w_gate, ffn2_w_up, ffn2_w_down,
              mix_norm_g, ssm_lam_re, ssm_lam_im, ssm_log_dt, ssm_b_re, ssm_b_im,
              ssm_c_re, ssm_c_im, ssm_d, ssm_w_glu_a, ssm_w_glu_b,
              kv_norm_g, w_k, w_v, k_norm_g, w_q, q_norm_g, rel_bias, w_o,
              ple_norm_g, ple_w_gate, ple_w_proj):
    weights = (ffn1_norm_g, ffn1_w_gate, ffn1_w_up, ffn1_w_down,
               ffn2_norm_g, ffn2_w_gate, ffn2_w_up, ffn2_w_down,
               mix_norm_g, ssm_lam_re, ssm_lam_im, ssm_log_dt, ssm_b_re, ssm_b_im,
               ssm_c_re, ssm_c_im, ssm_d, ssm_w_glu_a, ssm_w_glu_b,
               kv_norm_g, w_k, w_v, k_norm_g, w_q, q_norm_g, rel_bias, w_o,
               ple_norm_g, ple_w_gate, ple_w_proj)
    h0_prompt = jnp.zeros((N_A_LAYERS, x_prompt.shape[0], SSM_GROUPS, SSM_STATE), jnp.complex64)
    h0_sample = lax.complex(state_ssm_re.astype(jnp.float32), state_ssm_im.astype(jnp.float32))
    y_prompt, ssm_p, k_p, v_p = run_trunk(x_prompt, p_prompt, h0_prompt, None, None, weights)
    y_sample, ssm_s, k_s, v_s = run_trunk(x_sample, p_sample, h0_sample, cache_k, cache_v, weights)
    return (y_prompt, y_sample, jnp.real(ssm_p), jnp.imag(ssm_p), k_p, v_p,
            jnp.real(ssm_s), jnp.imag(ssm_s), k_s, v_s)
```

```python
import functools
import math

import jax
import jax.numpy as jnp
from jax import lax
from jax.experimental import pallas as pl
from jax.experimental.pallas import tpu as pltpu

F32 = jnp.float32
BF16 = jnp.bfloat16

D_MODEL = 2048
D_FF = 5632
N_HEADS = 16
HEAD_DIM = 128
CHUNK = 64
BAND_PAST = 512
REL_CLIP = 256
PLE_DIM = 256
SSM_GROUP = 16
SSM_GROUPS = 128
SSM_STATE = 64
RMS_EPS = 1e-6
MASK_VALUE = -1e30

V7X_VMEM_BYTES = 64 * 1024 * 1024
VMEM_LIMIT = 56 * 1024 * 1024
SUBLANES = 8
LANES = 128
N_LANE_BLOCKS = D_MODEL // LANES

TOKEN_TILE = 512
FF_TILE = 512
SSM_TIME_TILE = 64
SSM_GB = 8
SSM_GB_CH = D_MODEL // SSM_GB
SSM_GB_ST = (SSM_GROUPS // SSM_GB) * SSM_STATE


def _params(semantics):
    return pltpu.CompilerParams(dimension_semantics=semantics, vmem_limit_bytes=VMEM_LIMIT)


def _resident(shape):
    nd = len(shape)
    return pl.BlockSpec(shape, lambda *_: (0,) * nd, pipeline_mode=pl.Buffered(1))


def _rms(x, g):
    return x * lax.rsqrt(jnp.mean(x * x, axis=-1, keepdims=True) + RMS_EPS) * g


def _head_rms(x, g):
    parts = []
    for h in range(N_HEADS):
        xh = x[:, h * HEAD_DIM:(h + 1) * HEAD_DIM]
        parts.append(_rms(xh, g))
    return jnp.concatenate(parts, axis=-1)


def _ffn_kernel(x_ref, g_ref, wg_ref, wu_ref, wd_ref, o_ref, xn_ref, acc_ref, *, n_ff):
    f = pl.program_id(1)

    @pl.when(f == 0)
    def _():
        xn_ref[...] = _rms(x_ref[...], g_ref[...]).astype(BF16)
        acc_ref[...] = jnp.zeros_like(acc_ref)

    xn = xn_ref[...]
    gate = jnp.dot(xn, wg_ref[...], preferred_element_type=F32)
    up = jnp.dot(xn, wu_ref[...], preferred_element_type=F32)
    hid = (gate * jax.nn.sigmoid(gate)) * up
    acc_ref[...] += jnp.dot(hid.astype(BF16), wd_ref[...], preferred_element_type=F32)

    @pl.when(f == n_ff - 1)
    def _():
        o_ref[...] = x_ref[...] + 0.5 * acc_ref[...]


def _ffn(x, g, wg, wu, wd):
    t = x.shape[0]
    n_ff = D_FF // FF_TILE
    return pl.pallas_call(
        functools.partial(_ffn_kernel, n_ff=n_ff),
        grid=(t // TOKEN_TILE, n_ff),
        in_specs=[
            pl.BlockSpec((TOKEN_TILE, D_MODEL), lambda i, f: (i, 0)),
            pl.BlockSpec((1, D_MODEL), lambda i, f: (0, 0)),
            pl.BlockSpec((D_MODEL, FF_TILE), lambda i, f: (0, f)),
            pl.BlockSpec((D_MODEL, FF_TILE), lambda i, f: (0, f)),
            pl.BlockSpec((FF_TILE, D_MODEL), lambda i, f: (f, 0)),
        ],
        out_specs=pl.BlockSpec((TOKEN_TILE, D_MODEL), lambda i, f: (i, 0)),
        out_shape=jax.ShapeDtypeStruct((t, D_MODEL), F32),
        scratch_shapes=[pltpu.VMEM((TOKEN_TILE, D_MODEL), BF16),
                        pltpu.VMEM((TOKEN_TILE, D_MODEL), F32)],
        compiler_params=_params(("parallel", "arbitrary")),
        name="ffn",
    )(x, g.reshape(1, D_MODEL), wg, wu, wd)


def _glu_kernel(z_ref, h_ref, wa_ref, wb_ref, o_ref):
    z = z_ref[...]
    a = jnp.dot(z, wa_ref[...], preferred_element_type=F32)
    b = jnp.dot(z, wb_ref[...], preferred_element_type=F32)
    o_ref[...] = h_ref[...] + a * jax.nn.sigmoid(b)


def _glu(z, h, wa, wb):
    t = z.shape[0]
    row = lambda i: (i, 0)
    return pl.pallas_call(
        _glu_kernel,
        grid=(t // TOKEN_TILE,),
        in_specs=[pl.BlockSpec((TOKEN_TILE, D_MODEL), row),
                  pl.BlockSpec((TOKEN_TILE, D_MODEL), row),
                  _resident((D_MODEL, D_MODEL)),
                  _resident((D_MODEL, D_MODEL))],
        out_specs=pl.BlockSpec((TOKEN_TILE, D_MODEL), row),
        out_shape=jax.ShapeDtypeStruct((t, D_MODEL), F32),
        compiler_params=_params(("parallel",)),
        name="glu",
    )(z, h, wa, wb)


def _ple_kernel(h_ref, p_ref, g_ref, wgate_ref, wproj_ref, o_ref):
    h = h_ref[...]
    hn = _rms(h, g_ref[...]).astype(BF16)
    gate = jax.nn.sigmoid(jnp.dot(hn, wgate_ref[...], preferred_element_type=F32))
    proj = jnp.dot(p_ref[...].astype(BF16), wproj_ref[...], preferred_element_type=F32)
    o_ref[...] = h + proj * gate


def _ple(h, p, g, wgate, wproj):
    t = h.shape[0]
    row = lambda i: (i, 0)
    return pl.pallas_call(
        _ple_kernel,
        grid=(t // TOKEN_TILE,),
        in_specs=[pl.BlockSpec((TOKEN_TILE, D_MODEL), row),
                  pl.BlockSpec((TOKEN_TILE, PLE_DIM), row),
                  _resident((1, D_MODEL)),
                  _resident((D_MODEL, D_MODEL)),
                  _resident((PLE_DIM, D_MODEL))],
        out_specs=pl.BlockSpec((TOKEN_TILE, D_MODEL), row),
        out_shape=jax.ShapeDtypeStruct((t, D_MODEL), F32),
        compiler_params=_params(("parallel",)),
        name="ple",
    )(h, p, g.reshape(1, D_MODEL), wgate, wproj)


def _kv_kernel(h_ref, g_ref, wk_ref, wv_ref, kg_ref, k_ref, v_ref, kb_ref, vb_ref):
    hn = _rms(h_ref[...], g_ref[...]).astype(BF16)
    k = _head_rms(jnp.dot(hn, wk_ref[...], preferred_element_type=F32), kg_ref[...])
    v = jnp.dot(hn, wv_ref[...], preferred_element_type=F32)
    k_ref[...] = k
    v_ref[...] = v
    kb_ref[...] = k.astype(BF16)
    vb_ref[...] = v.astype(BF16)


def _shared_kv(h, g, wk, wv, kg):
    t = h.shape[0]
    row = lambda i: (i, 0)
    blk = pl.BlockSpec((TOKEN_TILE, D_MODEL), row)
    return pl.pallas_call(
        _kv_kernel,
        grid=(t // TOKEN_TILE,),
        in_specs=[blk, _resident((1, D_MODEL)), _resident((D_MODEL, D_MODEL)),
                  _resident((D_MODEL, D_MODEL)), _resident((1, HEAD_DIM))],
        out_specs=[blk, blk, blk, blk],
        out_shape=[jax.ShapeDtypeStruct((t, D_MODEL), F32),
                   jax.ShapeDtypeStruct((t, D_MODEL), F32),
                   jax.ShapeDtypeStruct((t, D_MODEL), BF16),
                   jax.ShapeDtypeStruct((t, D_MODEL), BF16)],
        compiler_params=_params(("parallel",)),
        name="shared_kv",
    )(h, g.reshape(1, D_MODEL), wk, wv, kg.reshape(1, HEAD_DIM))


def _q_kernel(h_ref, g_ref, wq_ref, qg_ref, q_ref):
    hn = _rms(h_ref[...], g_ref[...]).astype(BF16)
    q = _head_rms(jnp.dot(hn, wq_ref[...], preferred_element_type=F32), qg_ref[...])
    q_ref[...] = (q * (HEAD_DIM ** -0.5)).astype(BF16)


def _q_proj(h, g, wq, qg):
    t = h.shape[0]
    row = lambda i: (i, 0)
    blk = pl.BlockSpec((TOKEN_TILE, D_MODEL), row)
    return pl.pallas_call(
        _q_kernel,
        grid=(t // TOKEN_TILE,),
        in_specs=[blk, _resident((1, D_MODEL)), _resident((D_MODEL, D_MODEL)),
                  _resident((1, HEAD_DIM))],
        out_specs=blk,
        out_shape=jax.ShapeDtypeStruct((t, D_MODEL), BF16),
        compiler_params=_params(("parallel",)),
        name="q_proj",
    )(h, g.reshape(1, D_MODEL), wq, qg.reshape(1, HEAD_DIM))


def _oproj_kernel(o_ref, h_ref, wo_ref, out_ref):
    out_ref[...] = h_ref[...] + jnp.dot(o_ref[...], wo_ref[...], preferred_element_type=F32)


def _o_proj(o, h, wo):
    t = h.shape[0]
    row = lambda i: (i, 0)
    blk = pl.BlockSpec((TOKEN_TILE, D_MODEL), row)
    return pl.pallas_call(
        _oproj_kernel,
        grid=(t // TOKEN_TILE,),
        in_specs=[blk, blk, _resident((D_MODEL, D_MODEL))],
        out_specs=blk,
        out_shape=jax.ShapeDtypeStruct((t, D_MODEL), F32),
        compiler_params=_params(("parallel",)),
        name="o_proj",
    )(o, h, wo)


def _attend_chunk(q, kp, kc, vp, vc, bias_p_ref, bias_c_ref, valid_p):
    outs = []
    nt = (((1,), (1,)), ((), ()))
    for h in range(N_HEADS):
        sl = slice(h * HEAD_DIM, (h + 1) * HEAD_DIM)
        qh = q[:, sl]
        s_p = lax.dot_general(qh, kp[:, sl], nt, preferred_element_type=F32) + bias_p_ref[h]
        s_c = lax.dot_general(qh, kc[:, sl], nt, preferred_element_type=F32) + bias_c_ref[h]
        if valid_p is not None:
            s_p = jnp.where(valid_p, s_p, MASK_VALUE)
        m = jnp.maximum(jnp.max(s_p, axis=-1, keepdims=True),
                        jnp.max(s_c, axis=-1, keepdims=True))
        e_p = jnp.exp(s_p - m)
        e_c = jnp.exp(s_c - m)
        denom = jnp.sum(e_p, axis=-1, keepdims=True) + jnp.sum(e_c, axis=-1, keepdims=True)
        o = (jnp.dot((e_p / denom).astype(BF16), vp[:, sl], preferred_element_type=F32)
             + jnp.dot((e_c / denom).astype(BF16), vc[:, sl], preferred_element_type=F32))
        outs.append(o)
    return jnp.concatenate(outs, axis=-1)


def _attn_prompt_kernel(q_ref, k_ref, v_ref, bp_ref, bc_ref, o_ref):
    c = pl.program_id(1)
    start = pl.multiple_of(c * CHUNK, CHUNK)
    cur = pl.multiple_of(c * CHUNK + BAND_PAST, CHUNK)
    kp = k_ref[pl.ds(start, BAND_PAST), :]
    vp = v_ref[pl.ds(start, BAND_PAST), :]
    kc = k_ref[pl.ds(cur, CHUNK), :]
    vc = v_ref[pl.ds(cur, CHUNK), :]
    j = lax.broadcasted_iota(jnp.int32, (1, BAND_PAST), 1)
    valid_p = (c * CHUNK - BAND_PAST + j) >= 0
    o_ref[...] = _attend_chunk(q_ref[...], kp, kc, vp, vc, bp_ref, bc_ref, valid_p).astype(BF16)


def _attn_prompt(q, k_pad, v_pad, bias_p, bias_c):
    b, t, _ = q.shape
    tp = k_pad.shape[1]
    kv_spec = pl.BlockSpec((None, tp, D_MODEL), lambda i, c: (i, 0, 0),
                           pipeline_mode=pl.Buffered(1))
    return pl.pallas_call(
        _attn_prompt_kernel,
        grid=(b, t // CHUNK),
        in_specs=[pl.BlockSpec((None, CHUNK, D_MODEL), lambda i, c: (i, c, 0)),
                  kv_spec, kv_spec,
                  _resident((N_HEADS, CHUNK, BAND_PAST)),
                  _resident((N_HEADS, CHUNK, CHUNK))],
        out_specs=pl.BlockSpec((None, CHUNK, D_MODEL), lambda i, c: (i, c, 0)),
        out_shape=jax.ShapeDtypeStruct((b, t, D_MODEL), BF16),
        compiler_params=_params(("parallel", "arbitrary")),
        name="attn_prompt",
    )(q, k_pad, v_pad, bias_p, bias_c)


def _attn_sample_kernel(q_ref, kp_ref, vp_ref, kc_ref, vc_ref, bp_ref, bc_ref, o_ref):
    kp = kp_ref[...].astype(BF16)
    vp = vp_ref[...].astype(BF16)
    o_ref[...] = _attend_chunk(q_ref[...], kp, kc_ref[...], vp, vc_ref[...],
                               bp_ref, bc_ref, None).astype(BF16)


def _attn_sample(q, k_cache, v_cache, k_new, v_new, bias_p, bias_c):
    b = q.shape[0]
    new = pl.BlockSpec((None, CHUNK, D_MODEL), lambda i: (i, 0, 0))
    past = pl.BlockSpec((None, BAND_PAST, D_MODEL), lambda i: (i, 0, 0))
    return pl.pallas_call(
        _attn_sample_kernel,
        grid=(b,),
        in_specs=[new, past, past, new, new,
                  _resident((N_HEADS, CHUNK, BAND_PAST)),
                  _resident((N_HEADS, CHUNK, CHUNK))],
        out_specs=new,
        out_shape=jax.ShapeDtypeStruct((b, CHUNK, D_MODEL), BF16),
        compiler_params=_params(("parallel",)),
        name="attn_sample",
    )(q, k_cache, v_cache, k_new, v_new, bias_p, bias_c)


def _ssm_kernel(x_ref, g_ref, d_ref, wb_ref, wc_ref, are_ref, aim_ref, h0_ref,
                z_ref, hT_ref, u_ref, bu_ref, y_ref, st_ref, *, n_seq, n_t):
    ti = pl.program_id(1)
    tt = SSM_TIME_TILE
    rows = tt * SUBLANES

    @pl.when(ti == 0)
    def _():
        st_ref[...] = h0_ref[...]
        if n_seq < SUBLANES:
            u_ref[...] = jnp.zeros_like(u_ref)

    for j in range(n_seq):
        un = _rms(x_ref[j], g_ref[...])
        for k in range(N_LANE_BLOCKS):
            u_ref[k, pl.ds(j, tt, stride=SUBLANES), :] = un[:, k * LANES:(k + 1) * LANES]

    for gb in range(SSM_GB):
        kb = gb * (SSM_GB_CH // LANES)
        lhs = jnp.concatenate([u_ref[kb], u_ref[kb + 1]], axis=-1).astype(BF16)
        bu_ref[...] = jnp.dot(lhs, wb_ref[gb], preferred_element_type=F32)
        a_re = jnp.broadcast_to(are_ref[gb], (SUBLANES, SSM_GB_ST))
        a_im = jnp.broadcast_to(aim_ref[gb], (SUBLANES, SSM_GB_ST))

        def step(t, carry):
            h_re, h_im = carry
            r = pl.multiple_of(t * SUBLANES, SUBLANES)
            n_re = a_re * h_re - a_im * h_im + bu_ref[pl.ds(r, SUBLANES), :SSM_GB_ST]
            n_im = a_re * h_im + a_im * h_re + bu_ref[pl.ds(r, SUBLANES), SSM_GB_ST:]
            bu_ref[pl.ds(r, SUBLANES), :SSM_GB_ST] = n_re
            bu_ref[pl.ds(r, SUBLANES), SSM_GB_ST:] = n_im
            return n_re, n_im

        h_re, h_im = lax.fori_loop(
            0, tt, step, (st_ref[gb, :, :SSM_GB_ST], st_ref[gb, :, SSM_GB_ST:]))
        st_ref[gb, :, :SSM_GB_ST] = h_re
        st_ref[gb, :, SSM_GB_ST:] = h_im
        y = jnp.dot(bu_ref[...].astype(BF16), wc_ref[gb], preferred_element_type=F32)
        y_ref[kb] = y[:, :LANES]
        y_ref[kb + 1] = y[:, LANES:]

    for k in range(N_LANE_BLOCKS):
        lanes = slice(k * LANES, (k + 1) * LANES)
        y_ref[k] = jax.nn.gelu(y_ref[k] + d_ref[:, lanes] * u_ref[k])
        for j in range(n_seq):
            z_ref[j, :, lanes] = y_ref[k, pl.ds(j, tt, stride=SUBLANES), :].astype(BF16)

    @pl.when(ti == n_t - 1)
    def _():
        hT_ref[...] = st_ref[...]


def _ssm(x, g, d, wb, wc, a_re, a_im, h0, n_seq):
    b, t, _ = x.shape
    n_grp = b // n_seq
    n_t = t // SSM_TIME_TILE
    rows = SSM_TIME_TILE * SUBLANES
    st_block = (None, SSM_GB, SUBLANES, 2 * SSM_GB_ST)
    return pl.pallas_call(
        functools.partial(_ssm_kernel, n_seq=n_seq, n_t=n_t),
        grid=(n_grp, n_t),
        in_specs=[pl.BlockSpec((n_seq, SSM_TIME_TILE, D_MODEL), lambda s, i: (s, i, 0)),
                  _resident((1, D_MODEL)),
                  _resident((1, D_MODEL)),
                  _resident((SSM_GB, SSM_GB_CH, 2 * SSM_GB_ST)),
                  _resident((SSM_GB, 2 * SSM_GB_ST, SSM_GB_CH)),
                  _resident((SSM_GB, 1, SSM_GB_ST)),
                  _resident((SSM_GB, 1, SSM_GB_ST)),
                  pl.BlockSpec(st_block, lambda s, i: (s, 0, 0, 0))],
        out_specs=[pl.BlockSpec((n_seq, SSM_TIME_TILE, D_MODEL), lambda s, i: (s, i, 0)),
                   pl.BlockSpec(st_block, lambda s, i: (s, 0, 0, 0))],
        out_shape=[jax.ShapeDtypeStruct((b, t, D_MODEL), BF16),
                   jax.ShapeDtypeStruct((n_grp, SSM_GB, SUBLANES, 2 * SSM_GB_ST), F32)],
        scratch_shapes=[pltpu.VMEM((N_LANE_BLOCKS, rows, LANES), F32),
                        pltpu.VMEM((rows, 2 * SSM_GB_ST), F32),
                        pltpu.VMEM((N_LANE_BLOCKS, rows, LANES), F32),
                        pltpu.VMEM((SSM_GB, SUBLANES, 2 * SSM_GB_ST), F32)],
        compiler_params=_params(("parallel", "arbitrary")),
        name="ssm",
    )(x, g.reshape(1, D_MODEL), d.reshape(1, D_MODEL), wb, wc, a_re, a_im, h0)


def _ssm_weights(lam_re, lam_im, log_dt, b_re, b_im, c_re, c_im):
    lam = lax.complex(lam_re.astype(F32), lam_im.astype(F32))
    dt = jnp.exp(log_dt.astype(F32))[:, None]
    lam_bar = jnp.exp(lam * dt)
    b_bar = ((lam_bar - 1.0) / lam)[:, :, None] * lax.complex(b_re.astype(F32), b_im.astype(F32))
    gl = SSM_GROUPS // SSM_GB
    eye = jnp.eye(gl, dtype=F32)

    def in_proj(w):
        w = w.transpose(0, 2, 1).reshape(SSM_GB, gl, SSM_GROUP, SSM_STATE)
        return jnp.einsum('bgcp,gh->bgchp', w, eye).reshape(SSM_GB, SSM_GB_CH, SSM_GB_ST)

    def out_proj(w):
        w = w.transpose(0, 2, 1).reshape(SSM_GB, gl, SSM_STATE, SSM_GROUP)
        return jnp.einsum('bgpc,gh->bgphc', w, eye).reshape(SSM_GB, SSM_GB_ST, SSM_GB_CH)

    wb = jnp.concatenate([in_proj(jnp.real(b_bar)), in_proj(jnp.imag(b_bar))], axis=-1)
    wc = jnp.concatenate([out_proj(c_re.astype(F32)), out_proj(-c_im.astype(F32))], axis=1)
    a_re = jnp.real(lam_bar).reshape(SSM_GB, 1, SSM_GB_ST)
    a_im = jnp.imag(lam_bar).reshape(SSM_GB, 1, SSM_GB_ST)
    return wb.astype(BF16), wc.astype(BF16), a_re, a_im


def _state_to_blocks(h_re, h_im, n_seq):
    b = h_re.shape[0]
    def one(h):
        h = h.reshape(b // n_seq, n_seq, SSM_GB, SSM_GB_ST).transpose(0, 2, 1, 3)
        return jnp.pad(h, ((0, 0), (0, 0), (0, SUBLANES - n_seq), (0, 0)))
    return jnp.concatenate([one(h_re), one(h_im)], axis=-1)


def _blocks_to_state(st, n_seq):
    n_grp = st.shape[0]
    def one(h):
        h = h[:, :, :n_seq, :].transpose(0, 2, 1, 3)
        return h.reshape(1, n_grp * n_seq, SSM_GROUPS, SSM_STATE)
    return one(st[..., :SSM_GB_ST]), one(st[..., SSM_GB_ST:])


def _rel_bias(tab):
    t = jnp.arange(CHUNK)
    j = jnp.arange(BAND_PAST + CHUNK)
    idx = jnp.clip(BAND_PAST + t[:, None] - j[None, :], -REL_CLIP, REL_CLIP) + REL_CLIP
    bias = tab.astype(F32)[:, idx]
    return bias[:, :, :BAND_PAST], bias[:, :, BAND_PAST:]


def _trunk(x, p, ssm_h0, n_seq, cache, w):
    b, t, _ = x.shape
    n = b * t
    h = x.reshape(n, D_MODEL)

    h = _ffn(h, w['ffn1_norm_g'][0], *w['ffn1'][0])
    z, st = _ssm(h.reshape(b, t, D_MODEL), w['mix_norm_g'][0], w['ssm_d'][0],
                 w['ssm_wb'], w['ssm_wc'], w['ssm_a_re'], w['ssm_a_im'], ssm_h0, n_seq)
    h = _glu(z.reshape(n, D_MODEL), h, w['ssm_w_glu_a'], w['ssm_w_glu_b'])
    h = _ffn(h, w['ffn2_norm_g'][0], *w['ffn2'][0])
    h = _ple(h, p[0].reshape(n, PLE_DIM), w['ple_norm_g'][0], w['ple_w_gate'][0], w['ple_w_proj'][0])

    k, v, kb, vb = _shared_kv(h, w['kv_norm_g'], w['w_k'], w['w_v'], w['k_norm_g'])
    h = _ffn(h, w['ffn1_norm_g'][1], *w['ffn1'][1])
    q = _q_proj(h, w['mix_norm_g'][1], w['w_q'], w['q_norm_g']).reshape(b, t, D_MODEL)
    kb = kb.reshape(b, t, D_MODEL)
    vb = vb.reshape(b, t, D_MODEL)
    if cache is None:
        pad = ((0, 0), (BAND_PAST, 0), (0, 0))
        o = _attn_prompt(q, jnp.pad(kb, pad), jnp.pad(vb, pad), w['bias_p'], w['bias_c'])
    else:
        o = _attn_sample(q, cache[0], cache[1], kb, vb, w['bias_p'], w['bias_c'])
    h = _o_proj(o.reshape(n, D_MODEL), h, w['w_o'])
    h = _ffn(h, w['ffn2_norm_g'][1], *w['ffn2'][1])
    h = _ple(h, p[1].reshape(n, PLE_DIM), w['ple_norm_g'][1], w['ple_w_gate'][1], w['ple_w_proj'][1])
    return (h.reshape(b, t, D_MODEL), st,
            k.reshape(b, t, N_HEADS, HEAD_DIM), v.reshape(b, t, N_HEADS, HEAD_DIM))


def kernel(x_prompt, x_sample, state_ssm_re, state_ssm_im, cache_k, cache_v, p_prompt, p_sample, ffn1_norm_g, ffn1_w_gate, ffn1_w_up, ffn1_w_down, ffn2_norm_g, ffn2_w_gate, ffn2_w_up, ffn2_w_down, mix_norm_g, ssm_lam_re, ssm_lam_im, ssm_log_dt, ssm_b_re, ssm_b_im, ssm_c_re, ssm_c_im, ssm_d, ssm_w_glu_a, ssm_w_glu_b, kv_norm_g, w_k, w_v, k_norm_g, w_q, q_norm_g, rel_bias, w_o, ple_norm_g, ple_w_gate, ple_w_proj):
    bf = lambda a: a.astype(BF16)
    wb, wc, a_re, a_im = _ssm_weights(ssm_lam_re[0], ssm_lam_im[0], ssm_log_dt[0],
                                      ssm_b_re[0], ssm_b_im[0], ssm_c_re[0], ssm_c_im[0])
    bias_p, bias_c = _rel_bias(rel_bias[0])
    w = dict(
        ffn1_norm_g=ffn1_norm_g, ffn2_norm_g=ffn2_norm_g,
        ffn1=[(bf(ffn1_w_gate[i]), bf(ffn1_w_up[i]), bf(ffn1_w_down[i])) for i in range(2)],
        ffn2=[(bf(ffn2_w_gate[i]), bf(ffn2_w_up[i]), bf(ffn2_w_down[i])) for i in range(2)],
        mix_norm_g=mix_norm_g, ssm_d=ssm_d,
        ssm_wb=wb, ssm_wc=wc, ssm_a_re=a_re, ssm_a_im=a_im,
        ssm_w_glu_a=bf(ssm_w_glu_a[0]), ssm_w_glu_b=bf(ssm_w_glu_b[0]),
        kv_norm_g=kv_norm_g, w_k=bf(w_k), w_v=bf(w_v), k_norm_g=k_norm_g,
        w_q=bf(w_q[0]), q_norm_g=q_norm_g[0], bias_p=bias_p, bias_c=bias_c, w_o=bf(w_o[0]),
        ple_norm_g=ple_norm_g, ple_w_gate=[bf(ple_w_gate[i]) for i in range(2)],
        ple_w_proj=[bf(ple_w_proj[i]) for i in range(2)],
    )

    bp = x_prompt.shape[0]
    bs = x_sample.shape[0]
    n_seq_p = bp
    n_seq_s = SUBLANES
    zeros = jnp.zeros((bp, SSM_GROUPS, SSM_STATE), F32)
    h0_p = _state_to_blocks(zeros, zeros, n_seq_p)
    h0_s = _state_to_blocks(state_ssm_re[0].astype(F32), state_ssm_im[0].astype(F32), n_seq_s)

    y_p, st_p, k_p, v_p = _trunk(x_prompt, p_prompt, h0_p, n_seq_p, None, w)
    cache = (cache_k.reshape(bs, BAND_PAST, D_MODEL), cache_v.reshape(bs, BAND_PAST, D_MODEL))
    y_s, st_s, k_s, v_s = _trunk(x_sample, p_sample, h0_s, n_seq_s, cache, w)

    sp_re, sp_im = _blocks_to_state(st_p, n_seq_p)
    ss_re, ss_im = _blocks_to_state(st_s, n_seq_s)
    rows = min(BAND_PAST, x_prompt.shape[1])
    return (y_p, y_s, sp_re, sp_im, k_p[:, -rows:], v_p[:, -rows:],
            ss_re, ss_im, k_s, v_s)
```

```python
import functools
import math

import jax
import jax.numpy as jnp
from jax import lax
from jax.experimental import pallas as pl
from jax.experimental.pallas import tpu as pltpu

F32 = jnp.float32
BF16 = jnp.bfloat16

D_MODEL = 2048
D_FF = 5632
N_HEADS = 16
HEAD_DIM = 128
CHUNK = 64
BAND_PAST = 512
REL_CLIP = 256
PLE_DIM = 256
SSM_GROUP = 16
SSM_GROUPS = 128
SSM_STATE = 64
RMS_EPS = 1e-6
MASK_VALUE = -1e30

V7X_VMEM_BYTES = 64 * 1024 * 1024
VMEM_LIMIT = 56 * 1024 * 1024
SUBLANES = 8
LANES = 128
N_LANE_BLOCKS = D_MODEL // LANES

TOKEN_TILE = 512
FF_TILE = 512
KV_TILE = 256
ATTN_Q = 4 * CHUNK
ATTN_KV = BAND_PAST + ATTN_Q
SSM_TIME_TILE = 64
SSM_GB = 8
SSM_GB_CH = D_MODEL // SSM_GB
SSM_GB_ST = (SSM_GROUPS // SSM_GB) * SSM_STATE


def _params(semantics):
    return pltpu.CompilerParams(dimension_semantics=semantics, vmem_limit_bytes=VMEM_LIMIT)


def _resident(shape):
    nd = len(shape)
    return pl.BlockSpec(shape, lambda *_: (0,) * nd, pipeline_mode=pl.Buffered(1))


def _rms(x, g):
    return x * lax.rsqrt(jnp.mean(x * x, axis=-1, keepdims=True) + RMS_EPS) * g


def _head_rms(x, g):
    parts = []
    for h in range(N_HEADS):
        xh = x[:, h * HEAD_DIM:(h + 1) * HEAD_DIM]
        parts.append(_rms(xh, g))
    return jnp.concatenate(parts, axis=-1)


def _ffn_kernel(x_ref, g_ref, wg_ref, wu_ref, wd_ref, o_ref, xn_ref, acc_ref, *, n_ff):
    f = pl.program_id(1)

    @pl.when(f == 0)
    def _():
        xn_ref[...] = _rms(x_ref[...], g_ref[...]).astype(BF16)
        acc_ref[...] = jnp.zeros_like(acc_ref)

    xn = xn_ref[...]
    gate = jnp.dot(xn, wg_ref[...], preferred_element_type=F32)
    up = jnp.dot(xn, wu_ref[...], preferred_element_type=F32)
    hid = (gate * jax.nn.sigmoid(gate)) * up
    acc_ref[...] += jnp.dot(hid.astype(BF16), wd_ref[...], preferred_element_type=F32)

    @pl.when(f == n_ff - 1)
    def _():
        o_ref[...] = x_ref[...] + 0.5 * acc_ref[...]


def _ffn(x, g, wg, wu, wd):
    t = x.shape[0]
    n_ff = D_FF // FF_TILE
    return pl.pallas_call(
        functools.partial(_ffn_kernel, n_ff=n_ff),
        grid=(t // TOKEN_TILE, n_ff),
        in_specs=[
            pl.BlockSpec((TOKEN_TILE, D_MODEL), lambda i, f: (i, 0)),
            pl.BlockSpec((1, D_MODEL), lambda i, f: (0, 0)),
            pl.BlockSpec((D_MODEL, FF_TILE), lambda i, f: (0, f)),
            pl.BlockSpec((D_MODEL, FF_TILE), lambda i, f: (0, f)),
            pl.BlockSpec((FF_TILE, D_MODEL), lambda i, f: (f, 0)),
        ],
        out_specs=pl.BlockSpec((TOKEN_TILE, D_MODEL), lambda i, f: (i, 0)),
        out_shape=jax.ShapeDtypeStruct((t, D_MODEL), F32),
        scratch_shapes=[pltpu.VMEM((TOKEN_TILE, D_MODEL), BF16),
                        pltpu.VMEM((TOKEN_TILE, D_MODEL), F32)],
        compiler_params=_params(("parallel", "arbitrary")),
        name="ffn",
    )(x, g.reshape(1, D_MODEL), wg, wu, wd)


def _glu_kernel(z_ref, h_ref, wa_ref, wb_ref, o_ref):
    z = z_ref[...]
    a = jnp.dot(z, wa_ref[...], preferred_element_type=F32)
    b = jnp.dot(z, wb_ref[...], preferred_element_type=F32)
    o_ref[...] = h_ref[...] + a * jax.nn.sigmoid(b)


def _glu(z, h, wa, wb):
    t = z.shape[0]
    row = lambda i: (i, 0)
    return pl.pallas_call(
        _glu_kernel,
        grid=(t // TOKEN_TILE,),
        in_specs=[pl.BlockSpec((TOKEN_TILE, D_MODEL), row),
                  pl.BlockSpec((TOKEN_TILE, D_MODEL), row),
                  _resident((D_MODEL, D_MODEL)),
                  _resident((D_MODEL, D_MODEL))],
        out_specs=pl.BlockSpec((TOKEN_TILE, D_MODEL), row),
        out_shape=jax.ShapeDtypeStruct((t, D_MODEL), F32),
        compiler_params=_params(("parallel",)),
        name="glu",
    )(z, h, wa, wb)


def _ple_kernel(h_ref, p_ref, g_ref, wgate_ref, wproj_ref, o_ref):
    h = h_ref[...]
    hn = _rms(h, g_ref[...]).astype(BF16)
    gate = jax.nn.sigmoid(jnp.dot(hn, wgate_ref[...], preferred_element_type=F32))
    proj = jnp.dot(p_ref[...].astype(BF16), wproj_ref[...], preferred_element_type=F32)
    o_ref[...] = h + proj * gate


def _ple(h, p, g, wgate, wproj):
    t = h.shape[0]
    row = lambda i: (i, 0)
    return pl.pallas_call(
        _ple_kernel,
        grid=(t // TOKEN_TILE,),
        in_specs=[pl.BlockSpec((TOKEN_TILE, D_MODEL), row),
                  pl.BlockSpec((TOKEN_TILE, PLE_DIM), row),
                  _resident((1, D_MODEL)),
                  _resident((D_MODEL, D_MODEL)),
                  _resident((PLE_DIM, D_MODEL))],
        out_specs=pl.BlockSpec((TOKEN_TILE, D_MODEL), row),
        out_shape=jax.ShapeDtypeStruct((t, D_MODEL), F32),
        compiler_params=_params(("parallel",)),
        name="ple",
    )(h, p, g.reshape(1, D_MODEL), wgate, wproj)


def _kv_kernel(h_ref, g_ref, wk_ref, wv_ref, kg_ref, kb_ref, vb_ref, k4_ref, v4_ref, *,
               tiles_per_seq, kept_tiles):
    hn = _rms(h_ref[...], g_ref[...]).astype(BF16)
    k = _head_rms(jnp.dot(hn, wk_ref[...], preferred_element_type=F32), kg_ref[...])
    v = jnp.dot(hn, wv_ref[...], preferred_element_type=F32)
    kb_ref[...] = k.astype(BF16)
    vb_ref[...] = v.astype(BF16)

    @pl.when(lax.rem(pl.program_id(0), tiles_per_seq) >= tiles_per_seq - kept_tiles)
    def _():
        for h in range(N_HEADS):
            sl = slice(h * HEAD_DIM, (h + 1) * HEAD_DIM)
            k4_ref[pl.ds(h, KV_TILE, stride=N_HEADS), :] = k[:, sl]
            v4_ref[pl.ds(h, KV_TILE, stride=N_HEADS), :] = v[:, sl]


def _shared_kv(h, g, wk, wv, kg, tiles_per_seq, kept_tiles):
    t = h.shape[0]
    n_tiles = t // KV_TILE
    skipped = tiles_per_seq - kept_tiles
    row = lambda i: (i, 0)
    blk = pl.BlockSpec((KV_TILE, D_MODEL), row)

    def kept_block(i):
        return ((i // tiles_per_seq) * kept_tiles + jnp.maximum(i % tiles_per_seq - skipped, 0), 0)

    blk4 = pl.BlockSpec((KV_TILE * N_HEADS, HEAD_DIM), kept_block)
    kept = (n_tiles // tiles_per_seq) * kept_tiles * KV_TILE * N_HEADS
    return pl.pallas_call(
        functools.partial(_kv_kernel, tiles_per_seq=tiles_per_seq, kept_tiles=kept_tiles),
        grid=(n_tiles,),
        in_specs=[blk, _resident((1, D_MODEL)), _resident((D_MODEL, D_MODEL)),
                  _resident((D_MODEL, D_MODEL)), _resident((1, HEAD_DIM))],
        out_specs=[blk, blk, blk4, blk4],
        out_shape=[jax.ShapeDtypeStruct((t, D_MODEL), BF16),
                   jax.ShapeDtypeStruct((t, D_MODEL), BF16),
                   jax.ShapeDtypeStruct((kept, HEAD_DIM), F32),
                   jax.ShapeDtypeStruct((kept, HEAD_DIM), F32)],
        compiler_params=_params(("arbitrary",)),
        name="shared_kv",
    )(h, g.reshape(1, D_MODEL), wk, wv, kg.reshape(1, HEAD_DIM))


def _q_kernel(h_ref, g_ref, wq_ref, qg_ref, q_ref):
    hn = _rms(h_ref[...], g_ref[...]).astype(BF16)
    q = _head_rms(jnp.dot(hn, wq_ref[...], preferred_element_type=F32), qg_ref[...])
    q_ref[...] = (q * (HEAD_DIM ** -0.5)).astype(BF16)


def _q_proj(h, g, wq, qg):
    t = h.shape[0]
    row = lambda i: (i, 0)
    blk = pl.BlockSpec((TOKEN_TILE, D_MODEL), row)
    return pl.pallas_call(
        _q_kernel,
        grid=(t // TOKEN_TILE,),
        in_specs=[blk, _resident((1, D_MODEL)), _resident((D_MODEL, D_MODEL)),
                  _resident((1, HEAD_DIM))],
        out_specs=blk,
        out_shape=jax.ShapeDtypeStruct((t, D_MODEL), BF16),
        compiler_params=_params(("parallel",)),
        name="q_proj",
    )(h, g.reshape(1, D_MODEL), wq, qg.reshape(1, HEAD_DIM))


def _oproj_kernel(o_ref, h_ref, wo_ref, out_ref):
    out_ref[...] = h_ref[...] + jnp.dot(o_ref[...], wo_ref[...], preferred_element_type=F32)


def _o_proj(o, h, wo):
    t = h.shape[0]
    row = lambda i: (i, 0)
    blk = pl.BlockSpec((TOKEN_TILE, D_MODEL), row)
    return pl.pallas_call(
        _oproj_kernel,
        grid=(t // TOKEN_TILE,),
        in_specs=[blk, blk, _resident((D_MODEL, D_MODEL))],
        out_specs=blk,
        out_shape=jax.ShapeDtypeStruct((t, D_MODEL), F32),
        compiler_params=_params(("parallel",)),
        name="o_proj",
    )(o, h, wo)


def _attend_chunk(q, kp, kc, vp, vc, bias_p_ref, bias_c_ref, valid_p):
    outs = []
    nt = (((1,), (1,)), ((), ()))
    for h in range(N_HEADS):
        sl = slice(h * HEAD_DIM, (h + 1) * HEAD_DIM)
        qh = q[:, sl]
        s_p = lax.dot_general(qh, kp[:, sl], nt, preferred_element_type=F32) + bias_p_ref[h]
        s_c = lax.dot_general(qh, kc[:, sl], nt, preferred_element_type=F32) + bias_c_ref[h]
        if valid_p is not None:
            s_p = jnp.where(valid_p, s_p, MASK_VALUE)
        m = jnp.maximum(jnp.max(s_p, axis=-1, keepdims=True),
                        jnp.max(s_c, axis=-1, keepdims=True))
        e_p = jnp.exp(s_p - m)
        e_c = jnp.exp(s_c - m)
        denom = jnp.sum(e_p, axis=-1, keepdims=True) + jnp.sum(e_c, axis=-1, keepdims=True)
        o = (jnp.dot((e_p / denom).astype(BF16), vp[:, sl], preferred_element_type=F32)
             + jnp.dot((e_c / denom).astype(BF16), vc[:, sl], preferred_element_type=F32))
        outs.append(o)
    return jnp.concatenate(outs, axis=-1)


def _attn_prompt_kernel(q_ref, k0_ref, k1_ref, k2_ref, v0_ref, v1_ref, v2_ref, bias_ref, o_ref):
    g = pl.program_id(1)
    j = lax.broadcasted_iota(jnp.int32, (1, ATTN_KV), 1)
    valid = j >= (BAND_PAST - ATTN_Q * g)
    nt = (((1,), (1,)), ((), ()))
    outs = []
    for h in range(N_HEADS):
        sl = slice(h * HEAD_DIM, (h + 1) * HEAD_DIM)
        kh = jnp.concatenate([k0_ref[:, sl], k1_ref[:, sl], k2_ref[:, sl]], axis=0)
        vh = jnp.concatenate([v0_ref[:, sl], v1_ref[:, sl], v2_ref[:, sl]], axis=0)
        s = lax.dot_general(q_ref[:, sl], kh, nt, preferred_element_type=F32) + bias_ref[h]
        s = jnp.where(valid, s, MASK_VALUE)
        e = jnp.exp(s - jnp.max(s, axis=-1, keepdims=True))
        denom = jnp.sum(e, axis=-1, keepdims=True)
        outs.append(jnp.dot(e.astype(BF16), vh, preferred_element_type=F32) / denom)
    o_ref[...] = jnp.concatenate(outs, axis=-1).astype(BF16)


def _attn_prompt(q, k, v, bias):
    b, t, _ = q.shape
    n_past = BAND_PAST // ATTN_Q

    def kv_spec(m):
        return pl.BlockSpec((None, ATTN_Q, D_MODEL),
                            lambda i, g: (i, jnp.maximum(g - n_past + m, 0), 0))

    kv_specs = [kv_spec(m) for m in range(ATTN_KV // ATTN_Q)]
    return pl.pallas_call(
        _attn_prompt_kernel,
        grid=(b, t // ATTN_Q),
        in_specs=[pl.BlockSpec((None, ATTN_Q, D_MODEL), lambda i, g: (i, g, 0)),
                  *kv_specs, *kv_specs,
                  _resident((N_HEADS, ATTN_Q, ATTN_KV))],
        out_specs=pl.BlockSpec((None, ATTN_Q, D_MODEL), lambda i, g: (i, g, 0)),
        out_shape=jax.ShapeDtypeStruct((b, t, D_MODEL), BF16),
        compiler_params=_params(("parallel", "arbitrary")),
        name="attn_prompt",
    )(q, k, k, k, v, v, v, bias)


def _attn_sample_kernel(q_ref, kp_ref, vp_ref, kc_ref, vc_ref, bp_ref, bc_ref, o_ref, kp_s, vp_s):
    for h in range(N_HEADS):
        sl = slice(h * HEAD_DIM, (h + 1) * HEAD_DIM)
        kp_s[:, sl] = kp_ref[pl.ds(h, BAND_PAST, stride=N_HEADS), :].astype(BF16)
        vp_s[:, sl] = vp_ref[pl.ds(h, BAND_PAST, stride=N_HEADS), :].astype(BF16)
    o_ref[...] = _attend_chunk(q_ref[...], kp_s[...], kc_ref[...], vp_s[...], vc_ref[...],
                               bp_ref, bc_ref, None).astype(BF16)


def _attn_sample(q, k_cache, v_cache, k_new, v_new, bias_p, bias_c):
    b = q.shape[0]
    new = pl.BlockSpec((None, CHUNK, D_MODEL), lambda i: (i, 0, 0))
    past = pl.BlockSpec((None, BAND_PAST * N_HEADS, HEAD_DIM), lambda i: (i, 0, 0))
    return pl.pallas_call(
        _attn_sample_kernel,
        grid=(b,),
        in_specs=[new, past, past, new, new,
                  _resident((N_HEADS, CHUNK, BAND_PAST)),
                  _resident((N_HEADS, CHUNK, CHUNK))],
        out_specs=new,
        out_shape=jax.ShapeDtypeStruct((b, CHUNK, D_MODEL), BF16),
        scratch_shapes=[pltpu.VMEM((BAND_PAST, D_MODEL), BF16),
                        pltpu.VMEM((BAND_PAST, D_MODEL), BF16)],
        compiler_params=_params(("parallel",)),
        name="attn_sample",
    )(q, k_cache, v_cache, k_new, v_new, bias_p, bias_c)


def _ssm_kernel(x_ref, g_ref, d_ref, wb_ref, wc_ref, are_ref, aim_ref, h0_ref,
                z_ref, hT_ref, u_ref, bu_ref, y_ref, st_ref, *, n_seq, n_t):
    ti = pl.program_id(1)
    tt = SSM_TIME_TILE
    rows = tt * SUBLANES

    @pl.when(ti == 0)
    def _():
        st_ref[...] = h0_ref[...]
        if n_seq < SUBLANES:
            u_ref[...] = jnp.zeros_like(u_ref)

    for j in range(n_seq):
        un = _rms(x_ref[j], g_ref[...])
        for k in range(N_LANE_BLOCKS):
            u_ref[k, pl.ds(j, tt, stride=SUBLANES), :] = un[:, k * LANES:(k + 1) * LANES]

    for gb in range(SSM_GB):
        kb = gb * (SSM_GB_CH // LANES)
        lhs = jnp.concatenate([u_ref[kb], u_ref[kb + 1]], axis=-1).astype(BF16)
        bu_ref[...] = jnp.dot(lhs, wb_ref[gb], preferred_element_type=F32)
        a_re = jnp.broadcast_to(are_ref[gb], (SUBLANES, SSM_GB_ST))
        a_im = jnp.broadcast_to(aim_ref[gb], (SUBLANES, SSM_GB_ST))

        def step(t, carry):
            h_re, h_im = carry
            r = pl.multiple_of(t * SUBLANES, SUBLANES)
            n_re = a_re * h_re - a_im * h_im + bu_ref[pl.ds(r, SUBLANES), :SSM_GB_ST]
            n_im = a_re * h_im + a_im * h_re + bu_ref[pl.ds(r, SUBLANES), SSM_GB_ST:]
            bu_ref[pl.ds(r, SUBLANES), :SSM_GB_ST] = n_re
            bu_ref[pl.ds(r, SUBLANES), SSM_GB_ST:] = n_im
            return n_re, n_im

        h_re, h_im = lax.fori_loop(
            0, tt, step, (st_ref[gb, :, :SSM_GB_ST], st_ref[gb, :, SSM_GB_ST:]))
        st_ref[gb, :, :SSM_GB_ST] = h_re
        st_ref[gb, :, SSM_GB_ST:] = h_im
        y = jnp.dot(bu_ref[...].astype(BF16), wc_ref[gb], preferred_element_type=F32)
        y_ref[kb] = y[:, :LANES]
        y_ref[kb + 1] = y[:, LANES:]

    for k in range(N_LANE_BLOCKS):
        lanes = slice(k * LANES, (k + 1) * LANES)
        y_ref[k] = jax.nn.gelu(y_ref[k] + d_ref[:, lanes] * u_ref[k])
        for j in range(n_seq):
            z_ref[j, :, lanes] = y_ref[k, pl.ds(j, tt, stride=SUBLANES), :].astype(BF16)

    @pl.when(ti == n_t - 1)
    def _():
        hT_ref[...] = st_ref[...]


def _ssm(x, g, d, wb, wc, a_re, a_im, h0, n_seq):
    b, t, _ = x.shape
    n_grp = b // n_seq
    n_t = t // SSM_TIME_TILE
    rows = SSM_TIME_TILE * SUBLANES
    st_block = (None, SSM_GB, SUBLANES, 2 * SSM_GB_ST)
    return pl.pallas_call(
        functools.partial(_ssm_kernel, n_seq=n_seq, n_t=n_t),
        grid=(n_grp, n_t),
        in_specs=[pl.BlockSpec((n_seq, SSM_TIME_TILE, D_MODEL), lambda s, i: (s, i, 0)),
                  _resident((1, D_MODEL)),
                  _resident((1, D_MODEL)),
                  _resident((SSM_GB, SSM_GB_CH, 2 * SSM_GB_ST)),
                  _resident((SSM_GB, 2 * SSM_GB_ST, SSM_GB_CH)),
                  _resident((SSM_GB, 1, SSM_GB_ST)),
                  _resident((SSM_GB, 1, SSM_GB_ST)),
                  pl.BlockSpec(st_block, lambda s, i: (s, 0, 0, 0))],
        out_specs=[pl.BlockSpec((n_seq, SSM_TIME_TILE, D_MODEL), lambda s, i: (s, i, 0)),
                   pl.BlockSpec(st_block, lambda s, i: (s, 0, 0, 0))],
        out_shape=[jax.ShapeDtypeStruct((b, t, D_MODEL), BF16),
                   jax.ShapeDtypeStruct((n_grp, SSM_GB, SUBLANES, 2 * SSM_GB_ST), F32)],
        scratch_shapes=[pltpu.VMEM((N_LANE_BLOCKS, rows, LANES), F32),
                        pltpu.VMEM((rows, 2 * SSM_GB_ST), F32),
                        pltpu.VMEM((N_LANE_BLOCKS, rows, LANES), F32),
                        pltpu.VMEM((SSM_GB, SUBLANES, 2 * SSM_GB_ST), F32)],
        compiler_params=_params(("parallel", "arbitrary")),
        name="ssm",
    )(x, g.reshape(1, D_MODEL), d.reshape(1, D_MODEL), wb, wc, a_re, a_im, h0)


def _ssm_weights(lam_re, lam_im, log_dt, b_re, b_im, c_re, c_im):
    lam_re, lam_im = lam_re.astype(F32), lam_im.astype(F32)
    dt = jnp.exp(log_dt.astype(F32))[:, None]
    mag = jnp.exp(lam_re * dt)
    bar_re = mag * jnp.cos(lam_im * dt)
    bar_im = mag * jnp.sin(lam_im * dt)
    norm = lam_re * lam_re + lam_im * lam_im
    q_re = (((bar_re - 1.0) * lam_re + bar_im * lam_im) / norm)[:, :, None]
    q_im = ((bar_im * lam_re - (bar_re - 1.0) * lam_im) / norm)[:, :, None]
    b_re, b_im = b_re.astype(F32), b_im.astype(F32)
    bb_re = q_re * b_re - q_im * b_im
    bb_im = q_re * b_im + q_im * b_re
    gl = SSM_GROUPS // SSM_GB
    eye = jnp.eye(gl, dtype=F32)

    def in_proj(w):
        w = w.transpose(0, 2, 1).reshape(SSM_GB, gl, SSM_GROUP, SSM_STATE)
        return jnp.einsum('bgcp,gh->bgchp', w, eye).reshape(SSM_GB, SSM_GB_CH, SSM_GB_ST)

    def out_proj(w):
        w = w.transpose(0, 2, 1).reshape(SSM_GB, gl, SSM_STATE, SSM_GROUP)
        return jnp.einsum('bgpc,gh->bgphc', w, eye).reshape(SSM_GB, SSM_GB_ST, SSM_GB_CH)

    wb = jnp.concatenate([in_proj(bb_re), in_proj(bb_im)], axis=-1)
    wc = jnp.concatenate([out_proj(c_re.astype(F32)), out_proj(-c_im.astype(F32))], axis=1)
    a_re = bar_re.reshape(SSM_GB, 1, SSM_GB_ST)
    a_im = bar_im.reshape(SSM_GB, 1, SSM_GB_ST)
    return wb.astype(BF16), wc.astype(BF16), a_re, a_im


def _state_to_blocks(h_re, h_im, n_seq):
    b = h_re.shape[0]
    def one(h):
        h = h.reshape(b // n_seq, n_seq, SSM_GB, SSM_GB_ST).transpose(0, 2, 1, 3)
        return jnp.pad(h, ((0, 0), (0, 0), (0, SUBLANES - n_seq), (0, 0)))
    return jnp.concatenate([one(h_re), one(h_im)], axis=-1)


def _blocks_to_state(st, n_seq):
    n_grp = st.shape[0]
    def one(h):
        h = h[:, :, :n_seq, :].transpose(0, 2, 1, 3)
        return h.reshape(1, n_grp * n_seq, SSM_GROUPS, SSM_STATE)
    return one(st[..., :SSM_GB_ST]), one(st[..., SSM_GB_ST:])


def _rel_bias(tab):
    tab = tab.astype(F32)
    band = BAND_PAST + CHUNK
    n_far = band - REL_CLIP
    far = jnp.broadcast_to(tab[:, 2 * REL_CLIP:], (N_HEADS, n_far))
    near = tab[:, 2 * REL_CLIP - n_far:2 * REL_CLIP][:, ::-1]
    prof = jnp.concatenate([far, near], axis=1)
    period = prof.shape[1]
    skew = jnp.tile(prof, (1, CHUNK))[:, :CHUNK * (period - 1)].reshape(N_HEADS, CHUNK, period - 1)
    bias = skew[:, :, CHUNK - 1:CHUNK - 1 + band]
    rows = [jnp.pad(bias, ((0, 0), (0, 0), (c * CHUNK, ATTN_KV - band - c * CHUNK)),
                    constant_values=MASK_VALUE) for c in range(ATTN_Q // CHUNK)]
    return bias[:, :, :BAND_PAST], bias[:, :, BAND_PAST:], jnp.concatenate(rows, axis=1)


def _trunk(x, p, ssm_h0, n_seq, cache, w):
    b, t, _ = x.shape
    n = b * t
    h = x.reshape(n, D_MODEL)

    h = _ffn(h, w['ffn1_norm_g'][0], *w['ffn1'][0])
    z, st = _ssm(h.reshape(b, t, D_MODEL), w['mix_norm_g'][0], w['ssm_d'][0],
                 w['ssm_wb'], w['ssm_wc'], w['ssm_a_re'], w['ssm_a_im'], ssm_h0, n_seq)
    h = _glu(z.reshape(n, D_MODEL), h, w['ssm_w_glu_a'], w['ssm_w_glu_b'])
    h = _ffn(h, w['ffn2_norm_g'][0], *w['ffn2'][0])
    h = _ple(h, p[0].reshape(n, PLE_DIM), w['ple_norm_g'][0], w['ple_w_gate'][0], w['ple_w_proj'][0])

    rows = min(BAND_PAST, t)
    if t >= KV_TILE:
        assert rows % KV_TILE == 0 and t % KV_TILE == 0
        tiles_per_seq, kept_tiles = t // KV_TILE, rows // KV_TILE
    else:
        assert rows == t and KV_TILE % t == 0
        tiles_per_seq, kept_tiles = 1, 1
    kb, vb, k4, v4 = _shared_kv(h, w['kv_norm_g'], w['w_k'], w['w_v'], w['k_norm_g'],
                                tiles_per_seq, kept_tiles)
    h = _ffn(h, w['ffn1_norm_g'][1], *w['ffn1'][1])
    q = _q_proj(h, w['mix_norm_g'][1], w['w_q'], w['q_norm_g']).reshape(b, t, D_MODEL)
    kb = kb.reshape(b, t, D_MODEL)
    vb = vb.reshape(b, t, D_MODEL)
    if cache is None:
        o = _attn_prompt(q, kb, vb, w['bias_band'])
    else:
        o = _attn_sample(q, cache[0], cache[1], kb, vb, w['bias_p'], w['bias_c'])
    h = _o_proj(o.reshape(n, D_MODEL), h, w['w_o'])
    h = _ffn(h, w['ffn2_norm_g'][1], *w['ffn2'][1])
    h = _ple(h, p[1].reshape(n, PLE_DIM), w['ple_norm_g'][1], w['ple_w_gate'][1], w['ple_w_proj'][1])
    return (h.reshape(b, t, D_MODEL), st,
            k4.reshape(b, rows, N_HEADS, HEAD_DIM), v4.reshape(b, rows, N_HEADS, HEAD_DIM))


def kernel(x_prompt, x_sample, state_ssm_re, state_ssm_im, cache_k, cache_v, p_prompt, p_sample, ffn1_norm_g, ffn1_w_gate, ffn1_w_up, ffn1_w_down, ffn2_norm_g, ffn2_w_gate, ffn2_w_up, ffn2_w_down, mix_norm_g, ssm_lam_re, ssm_lam_im, ssm_log_dt, ssm_b_re, ssm_b_im, ssm_c_re, ssm_c_im, ssm_d, ssm_w_glu_a, ssm_w_glu_b, kv_norm_g, w_k, w_v, k_norm_g, w_q, q_norm_g, rel_bias, w_o, ple_norm_g, ple_w_gate, ple_w_proj):
    bf = lambda a: a.astype(BF16)
    wb, wc, a_re, a_im = _ssm_weights(ssm_lam_re[0], ssm_lam_im[0], ssm_log_dt[0],
                                      ssm_b_re[0], ssm_b_im[0], ssm_c_re[0], ssm_c_im[0])
    bias_p, bias_c, bias_band = _rel_bias(rel_bias[0])
    w = dict(
        ffn1_norm_g=ffn1_norm_g, ffn2_norm_g=ffn2_norm_g,
        ffn1=[(bf(ffn1_w_gate[i]), bf(ffn1_w_up[i]), bf(ffn1_w_down[i])) for i in range(2)],
        ffn2=[(bf(ffn2_w_gate[i]), bf(ffn2_w_up[i]), bf(ffn2_w_down[i])) for i in range(2)],
        mix_norm_g=mix_norm_g, ssm_d=ssm_d,
        ssm_wb=wb, ssm_wc=wc, ssm_a_re=a_re, ssm_a_im=a_im,
        ssm_w_glu_a=bf(ssm_w_glu_a[0]), ssm_w_glu_b=bf(ssm_w_glu_b[0]),
        kv_norm_g=kv_norm_g, w_k=bf(w_k), w_v=bf(w_v), k_norm_g=k_norm_g,
        w_q=bf(w_q[0]), q_norm_g=q_norm_g[0], bias_p=bias_p, bias_c=bias_c,
        bias_band=bias_band, w_o=bf(w_o[0]),
        ple_norm_g=ple_norm_g, ple_w_gate=[bf(ple_w_gate[i]) for i in range(2)],
        ple_w_proj=[bf(ple_w_proj[i]) for i in range(2)],
    )

    bp = x_prompt.shape[0]
    bs = x_sample.shape[0]
    n_seq_p = bp
    n_seq_s = SUBLANES
    zeros = jnp.zeros((bp, SSM_GROUPS, SSM_STATE), F32)
    h0_p = _state_to_blocks(zeros, zeros, n_seq_p)
    h0_s = _state_to_blocks(state_ssm_re[0].astype(F32), state_ssm_im[0].astype(F32), n_seq_s)

    y_p, st_p, k_p, v_p = _trunk(x_prompt, p_prompt, h0_p, n_seq_p, None, w)
    cache = (cache_k.reshape(bs, BAND_PAST * N_HEADS, HEAD_DIM),
             cache_v.reshape(bs, BAND_PAST * N_HEADS, HEAD_DIM))
    y_s, st_s, k_s, v_s = _trunk(x_sample, p_sample, h0_s, n_seq_s, cache, w)

    sp_re, sp_im = _blocks_to_state(st_p, n_seq_p)
    ss_re, ss_im = _blocks_to_state(st_s, n_seq_s)
    return (y_p, y_s, sp_re, sp_im, k_p, v_p, ss_re, ss_im, k_s, v_s)
```

```python
import functools
import math

import jax
import jax.numpy as jnp
from jax import lax
from jax.experimental import pallas as pl
from jax.experimental.pallas import tpu as pltpu

F32 = jnp.float32
BF16 = jnp.bfloat16

D_MODEL = 2048
D_FF = 5632
N_HEADS = 16
HEAD_DIM = 128
CHUNK = 64
BAND_PAST = 512
REL_CLIP = 256
PLE_DIM = 256
SSM_GROUP = 16
SSM_GROUPS = 128
SSM_STATE = 64
RMS_EPS = 1e-6
MASK_VALUE = -1e30

V7X_VMEM_BYTES = 64 * 1024 * 1024
VMEM_LIMIT = 56 * 1024 * 1024
SUBLANES = 8
BF16_SUBLANES = 16
LANES = 128
N_LANE_BLOCKS = D_MODEL // LANES

TOKEN_TILE = 512
CAST_TILE_ELEMS = 1 << 20
FFN_TOKEN_TILE = 1024
FF_TILE = 512
KV_TILE = 256
ATTN_Q = 4 * CHUNK
ATTN_KV = BAND_PAST + ATTN_Q
SSM_TIME_TILE = 64
SSM_GB = 8
SSM_BUFFERS = 3
SSM_GB_CH = D_MODEL // SSM_GB
SSM_GB_ST = (SSM_GROUPS // SSM_GB) * SSM_STATE


def _params(semantics):
    return pltpu.CompilerParams(dimension_semantics=semantics, vmem_limit_bytes=VMEM_LIMIT)


def _resident(shape):
    nd = len(shape)
    return pl.BlockSpec(shape, lambda *_: (0,) * nd, pipeline_mode=pl.Buffered(1))


def _rms(x, g):
    return x * lax.rsqrt(jnp.mean(x * x, axis=-1, keepdims=True) + RMS_EPS) * g


def _head_rms(x, g):
    parts = []
    for h in range(N_HEADS):
        xh = x[:, h * HEAD_DIM:(h + 1) * HEAD_DIM]
        parts.append(_rms(xh, g))
    return jnp.concatenate(parts, axis=-1)


def _cast_kernel(w_ref, o_ref):
    o_ref[...] = w_ref[...].astype(BF16)


def _to_bf16(w, layer=None):
    rows, cols = w.shape[-2:]
    tile = max(d for d in range(BF16_SUBLANES, rows + 1, BF16_SUBLANES)
               if rows % d == 0 and d * cols <= CAST_TILE_ELEMS)
    if layer is None:
        in_spec = pl.BlockSpec((tile, cols), lambda i: (i, 0))
    else:
        in_spec = pl.BlockSpec((None, tile, cols), lambda i: (layer, i, 0))
    return pl.pallas_call(
        _cast_kernel,
        grid=(rows // tile,),
        in_specs=[in_spec],
        out_specs=pl.BlockSpec((tile, cols), lambda i: (i, 0)),
        out_shape=jax.ShapeDtypeStruct((rows, cols), BF16),
        compiler_params=_params(("parallel",)),
        name="to_bf16",
    )(w)


def _ffn_kernel(x_ref, g_ref, wg_ref, wu_ref, wd_ref, o_ref, xn_ref):
    @pl.when(pl.program_id(1) == 0)
    def _():
        x = x_ref[...]
        xn_ref[...] = _rms(x, g_ref[...]).astype(BF16)
        o_ref[...] = x

    xn = xn_ref[...]
    gate = jnp.dot(xn, wg_ref[...], preferred_element_type=F32)
    up = jnp.dot(xn, wu_ref[...], preferred_element_type=F32)
    hid = (gate * jax.nn.sigmoid(gate)) * (0.5 * up)
    o_ref[...] += jnp.dot(hid.astype(BF16), wd_ref[...], preferred_element_type=F32)


def _ffn(x, g, wg, wu, wd):
    t = x.shape[0]
    return pl.pallas_call(
        _ffn_kernel,
        grid=(t // FFN_TOKEN_TILE, D_FF // FF_TILE),
        in_specs=[
            pl.BlockSpec((FFN_TOKEN_TILE, D_MODEL), lambda i, f: (i, 0)),
            pl.BlockSpec((1, D_MODEL), lambda i, f: (0, 0)),
            pl.BlockSpec((D_MODEL, FF_TILE), lambda i, f: (0, f)),
            pl.BlockSpec((D_MODEL, FF_TILE), lambda i, f: (0, f)),
            pl.BlockSpec((FF_TILE, D_MODEL), lambda i, f: (f, 0)),
        ],
        out_specs=pl.BlockSpec((FFN_TOKEN_TILE, D_MODEL), lambda i, f: (i, 0)),
        out_shape=jax.ShapeDtypeStruct((t, D_MODEL), F32),
        scratch_shapes=[pltpu.VMEM((FFN_TOKEN_TILE, D_MODEL), BF16)],
        compiler_params=_params(("parallel", "arbitrary")),
        name="ffn",
    )(x, g.reshape(1, D_MODEL), wg, wu, wd)


def _glu_kernel(z_ref, h_ref, wa_ref, wb_ref, o_ref):
    z = z_ref[...]
    a = jnp.dot(z, wa_ref[...], preferred_element_type=F32)
    b = jnp.dot(z, wb_ref[...], preferred_element_type=F32)
    o_ref[...] = h_ref[...] + a * jax.nn.sigmoid(b)


def _glu(z, h, wa, wb):
    t = z.shape[0]
    row = lambda i: (i, 0)
    return pl.pallas_call(
        _glu_kernel,
        grid=(t // TOKEN_TILE,),
        in_specs=[pl.BlockSpec((TOKEN_TILE, D_MODEL), row),
                  pl.BlockSpec((TOKEN_TILE, D_MODEL), row),
                  _resident((D_MODEL, D_MODEL)),
                  _resident((D_MODEL, D_MODEL))],
        out_specs=pl.BlockSpec((TOKEN_TILE, D_MODEL), row),
        out_shape=jax.ShapeDtypeStruct((t, D_MODEL), F32),
        compiler_params=_params(("parallel",)),
        name="glu",
    )(z, h, wa, wb)


def _ple_kernel(h_ref, p_ref, g_ref, wgate_ref, wproj_ref, o_ref):
    h = h_ref[...]
    hn = _rms(h, g_ref[...]).astype(BF16)
    gate = jax.nn.sigmoid(jnp.dot(hn, wgate_ref[...], preferred_element_type=F32))
    proj = jnp.dot(p_ref[...].astype(BF16), wproj_ref[...], preferred_element_type=F32)
    o_ref[...] = h + proj * gate


def _ple(h, p, g, wgate, wproj):
    t = h.shape[0]
    row = lambda i: (i, 0)
    return pl.pallas_call(
        _ple_kernel,
        grid=(t // TOKEN_TILE,),
        in_specs=[pl.BlockSpec((TOKEN_TILE, D_MODEL), row),
                  pl.BlockSpec((TOKEN_TILE, PLE_DIM), row),
                  _resident((1, D_MODEL)),
                  _resident((D_MODEL, D_MODEL)),
                  _resident((PLE_DIM, D_MODEL))],
        out_specs=pl.BlockSpec((TOKEN_TILE, D_MODEL), row),
        out_shape=jax.ShapeDtypeStruct((t, D_MODEL), F32),
        compiler_params=_params(("parallel",)),
        name="ple",
    )(h, p, g.reshape(1, D_MODEL), wgate, wproj)


def _kv_kernel(h_ref, g_ref, wk_ref, wv_ref, kg_ref, kb_ref, vb_ref, k4_ref, v4_ref, *,
               tiles_per_seq, kept_tiles):
    hn = _rms(h_ref[...], g_ref[...]).astype(BF16)
    k = _head_rms(jnp.dot(hn, wk_ref[...], preferred_element_type=F32), kg_ref[...])
    v = jnp.dot(hn, wv_ref[...], preferred_element_type=F32)
    kb_ref[...] = k.astype(BF16)
    vb_ref[...] = v.astype(BF16)

    @pl.when(lax.rem(pl.program_id(0), tiles_per_seq) >= tiles_per_seq - kept_tiles)
    def _():
        for h in range(N_HEADS):
            sl = slice(h * HEAD_DIM, (h + 1) * HEAD_DIM)
            k4_ref[pl.ds(h, KV_TILE, stride=N_HEADS), :] = k[:, sl]
            v4_ref[pl.ds(h, KV_TILE, stride=N_HEADS), :] = v[:, sl]


def _shared_kv(h, g, wk, wv, kg, tiles_per_seq, kept_tiles):
    t = h.shape[0]
    n_tiles = t // KV_TILE
    skipped = tiles_per_seq - kept_tiles
    row = lambda i: (i, 0)
    blk = pl.BlockSpec((KV_TILE, D_MODEL), row)

    def kept_block(i):
        return ((i // tiles_per_seq) * kept_tiles + jnp.maximum(i % tiles_per_seq - skipped, 0), 0)

    blk4 = pl.BlockSpec((KV_TILE * N_HEADS, HEAD_DIM), kept_block)
    kept = (n_tiles // tiles_per_seq) * kept_tiles * KV_TILE * N_HEADS
    return pl.pallas_call(
        functools.partial(_kv_kernel, tiles_per_seq=tiles_per_seq, kept_tiles=kept_tiles),
        grid=(n_tiles,),
        in_specs=[blk, _resident((1, D_MODEL)), _resident((D_MODEL, D_MODEL)),
                  _resident((D_MODEL, D_MODEL)), _resident((1, HEAD_DIM))],
        out_specs=[blk, blk, blk4, blk4],
        out_shape=[jax.ShapeDtypeStruct((t, D_MODEL), BF16),
                   jax.ShapeDtypeStruct((t, D_MODEL), BF16),
                   jax.ShapeDtypeStruct((kept, HEAD_DIM), F32),
                   jax.ShapeDtypeStruct((kept, HEAD_DIM), F32)],
        compiler_params=_params(("arbitrary",)),
        name="shared_kv",
    )(h, g.reshape(1, D_MODEL), wk, wv, kg.reshape(1, HEAD_DIM))


def _q_kernel(h_ref, g_ref, wq_ref, qg_ref, q_ref):
    hn = _rms(h_ref[...], g_ref[...]).astype(BF16)
    q = _head_rms(jnp.dot(hn, wq_ref[...], preferred_element_type=F32), qg_ref[...])
    q_ref[...] = (q * (HEAD_DIM ** -0.5)).astype(BF16)


def _q_proj(h, g, wq, qg):
    t = h.shape[0]
    row = lambda i: (i, 0)
    blk = pl.BlockSpec((TOKEN_TILE, D_MODEL), row)
    return pl.pallas_call(
        _q_kernel,
        grid=(t // TOKEN_TILE,),
        in_specs=[blk, _resident((1, D_MODEL)), _resident((D_MODEL, D_MODEL)),
                  _resident((1, HEAD_DIM))],
        out_specs=blk,
        out_shape=jax.ShapeDtypeStruct((t, D_MODEL), BF16),
        compiler_params=_params(("parallel",)),
        name="q_proj",
    )(h, g.reshape(1, D_MODEL), wq, qg.reshape(1, HEAD_DIM))


def _oproj_kernel(o_ref, h_ref, wo_ref, out_ref):
    out_ref[...] = h_ref[...] + jnp.dot(o_ref[...], wo_ref[...], preferred_element_type=F32)


def _o_proj(o, h, wo):
    t = h.shape[0]
    row = lambda i: (i, 0)
    blk = pl.BlockSpec((TOKEN_TILE, D_MODEL), row)
    return pl.pallas_call(
        _oproj_kernel,
        grid=(t // TOKEN_TILE,),
        in_specs=[blk, blk, _resident((D_MODEL, D_MODEL))],
        out_specs=blk,
        out_shape=jax.ShapeDtypeStruct((t, D_MODEL), F32),
        compiler_params=_params(("parallel",)),
        name="o_proj",
    )(o, h, wo)


def _attend_chunk(q, kp, kc, vp, vc, bias_p_ref, bias_c_ref, valid_p):
    outs = []
    nt = (((1,), (1,)), ((), ()))
    for h in range(N_HEADS):
        sl = slice(h * HEAD_DIM, (h + 1) * HEAD_DIM)
        qh = q[:, sl]
        s_p = lax.dot_general(qh, kp[:, sl], nt, preferred_element_type=F32) + bias_p_ref[h]
        s_c = lax.dot_general(qh, kc[:, sl], nt, preferred_element_type=F32) + bias_c_ref[h]
        if valid_p is not None:
            s_p = jnp.where(valid_p, s_p, MASK_VALUE)
        m = jnp.maximum(jnp.max(s_p, axis=-1, keepdims=True),
                        jnp.max(s_c, axis=-1, keepdims=True))
        e_p = jnp.exp(s_p - m)
        e_c = jnp.exp(s_c - m)
        denom = jnp.sum(e_p, axis=-1, keepdims=True) + jnp.sum(e_c, axis=-1, keepdims=True)
        o = (jnp.dot((e_p / denom).astype(BF16), vp[:, sl], preferred_element_type=F32)
             + jnp.dot((e_c / denom).astype(BF16), vc[:, sl], preferred_element_type=F32))
        outs.append(o)
    return jnp.concatenate(outs, axis=-1)


def _attn_prompt_kernel(q_ref, k0_ref, k1_ref, k2_ref, v0_ref, v1_ref, v2_ref, bias_ref, o_ref):
    g = pl.program_id(1)
    j = lax.broadcasted_iota(jnp.int32, (1, ATTN_KV), 1)
    valid = j >= (BAND_PAST - ATTN_Q * g)
    nt = (((1,), (1,)), ((), ()))
    outs = []
    for h in range(N_HEADS):
        sl = slice(h * HEAD_DIM, (h + 1) * HEAD_DIM)
        kh = jnp.concatenate([k0_ref[:, sl], k1_ref[:, sl], k2_ref[:, sl]], axis=0)
        vh = jnp.concatenate([v0_ref[:, sl], v1_ref[:, sl], v2_ref[:, sl]], axis=0)
        s = lax.dot_general(q_ref[:, sl], kh, nt, preferred_element_type=F32) + bias_ref[h]
        s = jnp.where(valid, s, MASK_VALUE)
        e = jnp.exp(s - jnp.max(s, axis=-1, keepdims=True))
        denom = jnp.sum(e, axis=-1, keepdims=True)
        outs.append(jnp.dot(e.astype(BF16), vh, preferred_element_type=F32) / denom)
    o_ref[...] = jnp.concatenate(outs, axis=-1).astype(BF16)


def _attn_prompt(q, k, v, bias):
    b, t, _ = q.shape
    n_past = BAND_PAST // ATTN_Q

    def kv_spec(m):
        return pl.BlockSpec((None, ATTN_Q, D_MODEL),
                            lambda i, g: (i, jnp.maximum(g - n_past + m, 0), 0))

    kv_specs = [kv_spec(m) for m in range(ATTN_KV // ATTN_Q)]
    return pl.pallas_call(
        _attn_prompt_kernel,
        grid=(b, t // ATTN_Q),
        in_specs=[pl.BlockSpec((None, ATTN_Q, D_MODEL), lambda i, g: (i, g, 0)),
                  *kv_specs, *kv_specs,
                  _resident((N_HEADS, ATTN_Q, ATTN_KV))],
        out_specs=pl.BlockSpec((None, ATTN_Q, D_MODEL), lambda i, g: (i, g, 0)),
        out_shape=jax.ShapeDtypeStruct((b, t, D_MODEL), BF16),
        compiler_params=_params(("parallel", "arbitrary")),
        name="attn_prompt",
    )(q, k, k, k, v, v, v, bias)


def _attn_sample_kernel(q_ref, kp_ref, vp_ref, kc_ref, vc_ref, bp_ref, bc_ref, o_ref, kp_s, vp_s):
    for h in range(N_HEADS):
        sl = slice(h * HEAD_DIM, (h + 1) * HEAD_DIM)
        kp_s[:, sl] = kp_ref[pl.ds(h, BAND_PAST, stride=N_HEADS), :].astype(BF16)
        vp_s[:, sl] = vp_ref[pl.ds(h, BAND_PAST, stride=N_HEADS), :].astype(BF16)
    o_ref[...] = _attend_chunk(q_ref[...], kp_s[...], kc_ref[...], vp_s[...], vc_ref[...],
                               bp_ref, bc_ref, None).astype(BF16)


def _attn_sample(q, k_cache, v_cache, k_new, v_new, bias_p, bias_c):
    b = q.shape[0]
    new = pl.BlockSpec((None, CHUNK, D_MODEL), lambda i: (i, 0, 0))
    past = pl.BlockSpec((None, BAND_PAST * N_HEADS, HEAD_DIM), lambda i: (i, 0, 0))
    return pl.pallas_call(
        _attn_sample_kernel,
        grid=(b,),
        in_specs=[new, past, past, new, new,
                  _resident((N_HEADS, CHUNK, BAND_PAST)),
                  _resident((N_HEADS, CHUNK, CHUNK))],
        out_specs=new,
        out_shape=jax.ShapeDtypeStruct((b, CHUNK, D_MODEL), BF16),
        scratch_shapes=[pltpu.VMEM((BAND_PAST, D_MODEL), BF16),
                        pltpu.VMEM((BAND_PAST, D_MODEL), BF16)],
        compiler_params=_params(("parallel",)),
        name="attn_sample",
    )(q, k_cache, v_cache, k_new, v_new, bias_p, bias_c)


def _ssm_kernel(x_ref, g_ref, d_ref, wb_ref, wc_ref, are_ref, aim_ref, h0_ref,
                z_ref, hT_ref, u_ref, bu_ref, y_ref, st_ref, sc_ref, *, n_seq, n_t):
    ti = pl.program_id(1)
    tt = SSM_TIME_TILE
    rows = tt * SUBLANES

    @pl.when(ti == 0)
    def _():
        st_ref[...] = h0_ref[...]
        if n_seq < SUBLANES:
            u_ref[...] = jnp.zeros_like(u_ref)

    for j in range(n_seq):
        x = x_ref[j]
        scale = lax.rsqrt(jnp.mean(x * x, axis=-1, keepdims=True) + RMS_EPS)
        sc_ref[j] = jnp.broadcast_to(scale, (tt, LANES))
        for k in range(N_LANE_BLOCKS):
            lanes = slice(k * LANES, (k + 1) * LANES)
            u_ref[k, pl.ds(j, tt, stride=SUBLANES), :] = x[:, lanes] * scale * g_ref[:, lanes]

    for gb in range(SSM_GB):
        kb = gb * (SSM_GB_CH // LANES)
        buf = bu_ref.at[gb % SSM_BUFFERS]
        lhs = jnp.concatenate([u_ref[kb], u_ref[kb + 1]], axis=-1).astype(BF16)
        buf[...] = jnp.dot(lhs, wb_ref[gb], preferred_element_type=F32)
        a_re = jnp.broadcast_to(are_ref[gb], (SUBLANES, SSM_GB_ST))
        a_im = jnp.broadcast_to(aim_ref[gb], (SUBLANES, SSM_GB_ST))
        h_re = st_ref[gb, :, :SSM_GB_ST]
        h_im = st_ref[gb, :, SSM_GB_ST:]
        for t in range(tt):
            r = slice(t * SUBLANES, (t + 1) * SUBLANES)
            n_re = a_re * h_re - a_im * h_im + buf[r, :SSM_GB_ST]
            n_im = a_re * h_im + a_im * h_re + buf[r, SSM_GB_ST:]
            buf[r, :SSM_GB_ST] = n_re
            buf[r, SSM_GB_ST:] = n_im
            h_re, h_im = n_re, n_im
        st_ref[gb, :, :SSM_GB_ST] = h_re
        st_ref[gb, :, SSM_GB_ST:] = h_im
        y = jnp.dot(buf[...].astype(BF16), wc_ref[gb], preferred_element_type=F32)
        y_ref[kb] = y[:, :LANES]
        y_ref[kb + 1] = y[:, LANES:]
        for k in (kb, kb + 1):
            lanes = slice(k * LANES, (k + 1) * LANES)
            for j in range(n_seq):
                u = x_ref[j, :, lanes] * sc_ref[j] * g_ref[:, lanes]
                yj = y_ref[k, pl.ds(j, tt, stride=SUBLANES), :]
                z_ref[j, :, lanes] = jax.nn.gelu(yj + d_ref[:, lanes] * u).astype(BF16)

    @pl.when(ti == n_t - 1)
    def _():
        hT_ref[...] = st_ref[...]


def _ssm(x, g, d, wb, wc, a_re, a_im, h0, n_seq):
    b, t, _ = x.shape
    n_grp = b // n_seq
    n_t = t // SSM_TIME_TILE
    rows = SSM_TIME_TILE * SUBLANES
    st_block = (None, SSM_GB, SUBLANES, 2 * SSM_GB_ST)
    return pl.pallas_call(
        functools.partial(_ssm_kernel, n_seq=n_seq, n_t=n_t),
        grid=(n_grp, n_t),
        in_specs=[pl.BlockSpec((n_seq, SSM_TIME_TILE, D_MODEL), lambda s, i: (s, i, 0)),
                  _resident((1, D_MODEL)),
                  _resident((1, D_MODEL)),
                  _resident((SSM_GB, SSM_GB_CH, 2 * SSM_GB_ST)),
                  _resident((SSM_GB, 2 * SSM_GB_ST, SSM_GB_CH)),
                  _resident((SSM_GB, 1, SSM_GB_ST)),
                  _resident((SSM_GB, 1, SSM_GB_ST)),
                  pl.BlockSpec(st_block, lambda s, i: (s, 0, 0, 0))],
        out_specs=[pl.BlockSpec((n_seq, SSM_TIME_TILE, D_MODEL), lambda s, i: (s, i, 0)),
                   pl.BlockSpec(st_block, lambda s, i: (s, 0, 0, 0))],
        out_shape=[jax.ShapeDtypeStruct((b, t, D_MODEL), BF16),
                   jax.ShapeDtypeStruct((n_grp, SSM_GB, SUBLANES, 2 * SSM_GB_ST), F32)],
        scratch_shapes=[pltpu.VMEM((N_LANE_BLOCKS, rows, LANES), F32),
                        pltpu.VMEM((SSM_BUFFERS, rows, 2 * SSM_GB_ST), F32),
                        pltpu.VMEM((N_LANE_BLOCKS, rows, LANES), F32),
                        pltpu.VMEM((SSM_GB, SUBLANES, 2 * SSM_GB_ST), F32),
                        pltpu.VMEM((n_seq, SSM_TIME_TILE, LANES), F32)],
        compiler_params=_params(("parallel", "arbitrary")),
        name="ssm",
    )(x, g.reshape(1, D_MODEL), d.reshape(1, D_MODEL), wb, wc, a_re, a_im, h0)


def _ssm_weights(lam_re, lam_im, log_dt, b_re, b_im, c_re, c_im):
    lam_re, lam_im = lam_re.astype(F32), lam_im.astype(F32)
    dt = jnp.exp(log_dt.astype(F32))[:, None]
    mag = jnp.exp(lam_re * dt)
    bar_re = mag * jnp.cos(lam_im * dt)
    bar_im = mag * jnp.sin(lam_im * dt)
    norm = lam_re * lam_re + lam_im * lam_im
    q_re = (((bar_re - 1.0) * lam_re + bar_im * lam_im) / norm)[:, :, None]
    q_im = ((bar_im * lam_re - (bar_re - 1.0) * lam_im) / norm)[:, :, None]
    b_re, b_im = b_re.astype(F32), b_im.astype(F32)
    bb_re = q_re * b_re - q_im * b_im
    bb_im = q_re * b_im + q_im * b_re
    gl = SSM_GROUPS // SSM_GB
    eye = jnp.eye(gl, dtype=F32)

    def in_proj(w):
        w = w.transpose(0, 2, 1).reshape(SSM_GB, gl, SSM_GROUP, SSM_STATE)
        return jnp.einsum('bgcp,gh->bgchp', w, eye).reshape(SSM_GB, SSM_GB_CH, SSM_GB_ST)

    def out_proj(w):
        w = w.transpose(0, 2, 1).reshape(SSM_GB, gl, SSM_STATE, SSM_GROUP)
        return jnp.einsum('bgpc,gh->bgphc', w, eye).reshape(SSM_GB, SSM_GB_ST, SSM_GB_CH)

    wb = jnp.concatenate([in_proj(bb_re), in_proj(bb_im)], axis=-1)
    wc = jnp.concatenate([out_proj(c_re.astype(F32)), out_proj(-c_im.astype(F32))], axis=1)
    a_re = bar_re.reshape(SSM_GB, 1, SSM_GB_ST)
    a_im = bar_im.reshape(SSM_GB, 1, SSM_GB_ST)
    return wb.astype(BF16), wc.astype(BF16), a_re, a_im


def _state_to_blocks(h_re, h_im, n_seq):
    b = h_re.shape[0]
    def one(h):
        h = h.reshape(b // n_seq, n_seq, SSM_GB, SSM_GB_ST).transpose(0, 2, 1, 3)
        return jnp.pad(h, ((0, 0), (0, 0), (0, SUBLANES - n_seq), (0, 0)))
    return jnp.concatenate([one(h_re), one(h_im)], axis=-1)


def _blocks_to_state(st, n_seq):
    n_grp = st.shape[0]
    def one(h):
        h = h[:, :, :n_seq, :].transpose(0, 2, 1, 3)
        return h.reshape(1, n_grp * n_seq, SSM_GROUPS, SSM_STATE)
    return one(st[..., :SSM_GB_ST]), one(st[..., SSM_GB_ST:])


def _rel_bias(tab):
    tab = tab.astype(F32)
    band = BAND_PAST + CHUNK
    n_far = band - REL_CLIP
    far = jnp.broadcast_to(tab[:, 2 * REL_CLIP:], (N_HEADS, n_far))
    near = tab[:, 2 * REL_CLIP - n_far:2 * REL_CLIP][:, ::-1]
    prof = jnp.concatenate([far, near], axis=1)
    period = prof.shape[1]
    skew = jnp.tile(prof, (1, CHUNK))[:, :CHUNK * (period - 1)].reshape(N_HEADS, CHUNK, period - 1)
    bias = skew[:, :, CHUNK - 1:CHUNK - 1 + band]
    rows = [jnp.pad(bias, ((0, 0), (0, 0), (c * CHUNK, ATTN_KV - band - c * CHUNK)),
                    constant_values=MASK_VALUE) for c in range(ATTN_Q // CHUNK)]
    return bias[:, :, :BAND_PAST], bias[:, :, BAND_PAST:], jnp.concatenate(rows, axis=1)


def _trunk(x, p, ssm_h0, n_seq, cache, w):
    b, t, _ = x.shape
    n = b * t
    h = x.reshape(n, D_MODEL)

    h = _ffn(h, w['ffn1_norm_g'][0], *w['ffn1'][0])
    z, st = _ssm(h.reshape(b, t, D_MODEL), w['mix_norm_g'][0], w['ssm_d'][0],
                 w['ssm_wb'], w['ssm_wc'], w['ssm_a_re'], w['ssm_a_im'], ssm_h0, n_seq)
    h = _glu(z.reshape(n, D_MODEL), h, w['ssm_w_glu_a'], w['ssm_w_glu_b'])
    h = _ffn(h, w['ffn2_norm_g'][0], *w['ffn2'][0])
    h = _ple(h, p[0].reshape(n, PLE_DIM), w['ple_norm_g'][0], w['ple_w_gate'][0], w['ple_w_proj'][0])

    rows = min(BAND_PAST, t)
    if t >= KV_TILE:
        assert rows % KV_TILE == 0 and t % KV_TILE == 0
        tiles_per_seq, kept_tiles = t // KV_TILE, rows // KV_TILE
    else:
        assert rows == t and KV_TILE % t == 0
        tiles_per_seq, kept_tiles = 1, 1
    kb, vb, k4, v4 = _shared_kv(h, w['kv_norm_g'], w['w_k'], w['w_v'], w['k_norm_g'],
                                tiles_per_seq, kept_tiles)
    h = _ffn(h, w['ffn1_norm_g'][1], *w['ffn1'][1])
    q = _q_proj(h, w['mix_norm_g'][1], w['w_q'], w['q_norm_g']).reshape(b, t, D_MODEL)
    kb = kb.reshape(b, t, D_MODEL)
    vb = vb.reshape(b, t, D_MODEL)
    if cache is None:
        o = _attn_prompt(q, kb, vb, w['bias_band'])
    else:
        o = _attn_sample(q, cache[0], cache[1], kb, vb, w['bias_p'], w['bias_c'])
    h = _o_proj(o.reshape(n, D_MODEL), h, w['w_o'])
    h = _ffn(h, w['ffn2_norm_g'][1], *w['ffn2'][1])
    h = _ple(h, p[1].reshape(n, PLE_DIM), w['ple_norm_g'][1], w['ple_w_gate'][1], w['ple_w_proj'][1])
    return (h.reshape(b, t, D_MODEL), st,
            k4.reshape(b, rows, N_HEADS, HEAD_DIM), v4.reshape(b, rows, N_HEADS, HEAD_DIM))


def kernel(x_prompt, x_sample, state_ssm_re, state_ssm_im, cache_k, cache_v, p_prompt, p_sample, ffn1_norm_g, ffn1_w_gate, ffn1_w_up, ffn1_w_down, ffn2_norm_g, ffn2_w_gate, ffn2_w_up, ffn2_w_down, mix_norm_g, ssm_lam_re, ssm_lam_im, ssm_log_dt, ssm_b_re, ssm_b_im, ssm_c_re, ssm_c_im, ssm_d, ssm_w_glu_a, ssm_w_glu_b, kv_norm_g, w_k, w_v, k_norm_g, w_q, q_norm_g, rel_bias, w_o, ple_norm_g, ple_w_gate, ple_w_proj):
    bf = _to_bf16
    wb, wc, a_re, a_im = _ssm_weights(ssm_lam_re[0], ssm_lam_im[0], ssm_log_dt[0],
                                      ssm_b_re[0], ssm_b_im[0], ssm_c_re[0], ssm_c_im[0])
    bias_p, bias_c, bias_band = _rel_bias(rel_bias[0])
    w = dict(
        ffn1_norm_g=ffn1_norm_g, ffn2_norm_g=ffn2_norm_g,
        ffn1=[(bf(ffn1_w_gate, i), bf(ffn1_w_up, i), bf(ffn1_w_down, i)) for i in range(2)],
        ffn2=[(bf(ffn2_w_gate, i), bf(ffn2_w_up, i), bf(ffn2_w_down, i)) for i in range(2)],
        mix_norm_g=mix_norm_g, ssm_d=ssm_d,
        ssm_wb=wb, ssm_wc=wc, ssm_a_re=a_re, ssm_a_im=a_im,
        ssm_w_glu_a=bf(ssm_w_glu_a, 0), ssm_w_glu_b=bf(ssm_w_glu_b, 0),
        kv_norm_g=kv_norm_g, w_k=bf(w_k), w_v=bf(w_v), k_norm_g=k_norm_g,
        w_q=bf(w_q, 0), q_norm_g=q_norm_g[0], bias_p=bias_p, bias_c=bias_c,
        bias_band=bias_band, w_o=bf(w_o, 0),
        ple_norm_g=ple_norm_g, ple_w_gate=[bf(ple_w_gate, i) for i in range(2)],
        ple_w_proj=[bf(ple_w_proj, i) for i in range(2)],
    )

    bp = x_prompt.shape[0]
    bs = x_sample.shape[0]
    n_seq_p = bp
    n_seq_s = SUBLANES
    zeros = jnp.zeros((bp, SSM_GROUPS, SSM_STATE), F32)
    h0_p = _state_to_blocks(zeros, zeros, n_seq_p)
    h0_s = _state_to_blocks(state_ssm_re[0].astype(F32), state_ssm_im[0].astype(F32), n_seq_s)

    y_p, st_p, k_p, v_p = _trunk(x_prompt, p_prompt, h0_p, n_seq_p, None, w)
    cache = (cache_k.reshape(bs, BAND_PAST * N_HEADS, HEAD_DIM),
             cache_v.reshape(bs, BAND_PAST * N_HEADS, HEAD_DIM))
    y_s, st_s, k_s, v_s = _trunk(x_sample, p_sample, h0_s, n_seq_s, cache, w)

    sp_re, sp_im = _blocks_to_state(st_p, n_seq_p)
    ss_re, ss_im = _blocks_to_state(st_s, n_seq_s)
    return (y_p, y_s, sp_re, sp_im, k_p, v_p, ss_re, ss_im, k_s, v_s)
```

```python
import functools
import math

import jax
import jax.numpy as jnp
from jax import lax
from jax.experimental import pallas as pl
from jax.experimental.pallas import tpu as pltpu

F32 = jnp.float32
BF16 = jnp.bfloat16

D_MODEL = 2048
D_FF = 5632
N_HEADS = 16
HEAD_DIM = 128
CHUNK = 64
BAND_PAST = 512
REL_CLIP = 256
PLE_DIM = 256
SSM_GROUP = 16
SSM_GROUPS = 128
SSM_STATE = 64
RMS_EPS = 1e-6
MASK_VALUE = -1e30

V7X_VMEM_BYTES = 64 * 1024 * 1024
VMEM_LIMIT = 56 * 1024 * 1024
SUBLANES = 8
BF16_SUBLANES = 16
LANES = 128
N_LANE_BLOCKS = D_MODEL // LANES

TOKEN_TILE = 512
CAST_TILE_ELEMS = 1 << 20
FFN_TOKEN_TILE = 1024
FF_TILE = 512
KV_TILE = 256
ATTN_Q = 4 * CHUNK
ATTN_KV = BAND_PAST + ATTN_Q
ATTN_ROWS = 32
SSM_TIME_TILE = 64
SSM_GB = 8
SSM_BUFFERS = 3
SSM_GB_CH = D_MODEL // SSM_GB
SSM_GB_ST = (SSM_GROUPS // SSM_GB) * SSM_STATE


def _params(semantics):
    return pltpu.CompilerParams(dimension_semantics=semantics, vmem_limit_bytes=VMEM_LIMIT)


def _resident(shape):
    nd = len(shape)
    return pl.BlockSpec(shape, lambda *_: (0,) * nd, pipeline_mode=pl.Buffered(1))


def _rms(x, g):
    return x * lax.rsqrt(jnp.mean(x * x, axis=-1, keepdims=True) + RMS_EPS) * g


def _head_rms(x, g):
    parts = []
    for h in range(N_HEADS):
        xh = x[:, h * HEAD_DIM:(h + 1) * HEAD_DIM]
        parts.append(_rms(xh, g))
    return jnp.concatenate(parts, axis=-1)


def _cast_kernel(w_ref, o_ref):
    o_ref[...] = w_ref[...].astype(BF16)


def _to_bf16(w, layer=None):
    rows, cols = w.shape[-2:]
    tile = max(d for d in range(BF16_SUBLANES, rows + 1, BF16_SUBLANES)
               if rows % d == 0 and d * cols <= CAST_TILE_ELEMS)
    if layer is None:
        in_spec = pl.BlockSpec((tile, cols), lambda i: (i, 0))
    else:
        in_spec = pl.BlockSpec((None, tile, cols), lambda i: (layer, i, 0))
    return pl.pallas_call(
        _cast_kernel,
        grid=(rows // tile,),
        in_specs=[in_spec],
        out_specs=pl.BlockSpec((tile, cols), lambda i: (i, 0)),
        out_shape=jax.ShapeDtypeStruct((rows, cols), BF16),
        compiler_params=_params(("parallel",)),
        name="to_bf16",
    )(w)


def _ffn_kernel(x_ref, g_ref, wg_ref, wu_ref, wd_ref, o_ref, xn_ref):
    @pl.when(pl.program_id(1) == 0)
    def _():
        x = x_ref[...]
        xn_ref[...] = _rms(x, g_ref[...]).astype(BF16)
        o_ref[...] = x

    xn = xn_ref[...]
    gate = jnp.dot(xn, wg_ref[...], preferred_element_type=F32)
    up = jnp.dot(xn, wu_ref[...], preferred_element_type=F32)
    hid = (gate * jax.nn.sigmoid(gate)) * (0.5 * up)
    o_ref[...] += jnp.dot(hid.astype(BF16), wd_ref[...], preferred_element_type=F32)


def _ffn(x, g, wg, wu, wd):
    t = x.shape[0]
    return pl.pallas_call(
        _ffn_kernel,
        grid=(t // FFN_TOKEN_TILE, D_FF // FF_TILE),
        in_specs=[
            pl.BlockSpec((FFN_TOKEN_TILE, D_MODEL), lambda i, f: (i, 0)),
            pl.BlockSpec((1, D_MODEL), lambda i, f: (0, 0)),
            pl.BlockSpec((D_MODEL, FF_TILE), lambda i, f: (0, f)),
            pl.BlockSpec((D_MODEL, FF_TILE), lambda i, f: (0, f)),
            pl.BlockSpec((FF_TILE, D_MODEL), lambda i, f: (f, 0)),
        ],
        out_specs=pl.BlockSpec((FFN_TOKEN_TILE, D_MODEL), lambda i, f: (i, 0)),
        out_shape=jax.ShapeDtypeStruct((t, D_MODEL), F32),
        scratch_shapes=[pltpu.VMEM((FFN_TOKEN_TILE, D_MODEL), BF16)],
        compiler_params=_params(("parallel", "arbitrary")),
        name="ffn",
    )(x, g.reshape(1, D_MODEL), wg, wu, wd)


def _glu_kernel(z_ref, h_ref, wa_ref, wb_ref, o_ref):
    z = z_ref[...]
    a = jnp.dot(z, wa_ref[...], preferred_element_type=F32)
    b = jnp.dot(z, wb_ref[...], preferred_element_type=F32)
    o_ref[...] = h_ref[...] + a * jax.nn.sigmoid(b)


def _glu(z, h, wa, wb):
    t = z.shape[0]
    row = lambda i: (i, 0)
    return pl.pallas_call(
        _glu_kernel,
        grid=(t // TOKEN_TILE,),
        in_specs=[pl.BlockSpec((TOKEN_TILE, D_MODEL), row),
                  pl.BlockSpec((TOKEN_TILE, D_MODEL), row),
                  _resident((D_MODEL, D_MODEL)),
                  _resident((D_MODEL, D_MODEL))],
        out_specs=pl.BlockSpec((TOKEN_TILE, D_MODEL), row),
        out_shape=jax.ShapeDtypeStruct((t, D_MODEL), F32),
        compiler_params=_params(("parallel",)),
        name="glu",
    )(z, h, wa, wb)


def _ple_kernel(h_ref, p_ref, g_ref, wgate_ref, wproj_ref, o_ref):
    h = h_ref[...]
    hn = _rms(h, g_ref[...]).astype(BF16)
    gate = jax.nn.sigmoid(jnp.dot(hn, wgate_ref[...], preferred_element_type=F32))
    proj = jnp.dot(p_ref[...].astype(BF16), wproj_ref[...], preferred_element_type=F32)
    o_ref[...] = h + proj * gate


def _ple(h, p, g, wgate, wproj):
    t = h.shape[0]
    row = lambda i: (i, 0)
    return pl.pallas_call(
        _ple_kernel,
        grid=(t // TOKEN_TILE,),
        in_specs=[pl.BlockSpec((TOKEN_TILE, D_MODEL), row),
                  pl.BlockSpec((TOKEN_TILE, PLE_DIM), row),
                  _resident((1, D_MODEL)),
                  _resident((D_MODEL, D_MODEL)),
                  _resident((PLE_DIM, D_MODEL))],
        out_specs=pl.BlockSpec((TOKEN_TILE, D_MODEL), row),
        out_shape=jax.ShapeDtypeStruct((t, D_MODEL), F32),
        compiler_params=_params(("parallel",)),
        name="ple",
    )(h, p, g.reshape(1, D_MODEL), wgate, wproj)


def _kv_kernel(h_ref, g_ref, wk_ref, wv_ref, kg_ref, kb_ref, vb_ref, k4_ref, v4_ref, *,
               tiles_per_seq, kept_tiles):
    hn = _rms(h_ref[...], g_ref[...]).astype(BF16)
    k = _head_rms(jnp.dot(hn, wk_ref[...], preferred_element_type=F32), kg_ref[...])
    v = jnp.dot(hn, wv_ref[...], preferred_element_type=F32)
    kb_ref[...] = k.astype(BF16)
    vb_ref[...] = v.astype(BF16)

    @pl.when(lax.rem(pl.program_id(0), tiles_per_seq) >= tiles_per_seq - kept_tiles)
    def _():
        for h in range(N_HEADS):
            sl = slice(h * HEAD_DIM, (h + 1) * HEAD_DIM)
            k4_ref[pl.ds(h, KV_TILE, stride=N_HEADS), :] = k[:, sl]
            v4_ref[pl.ds(h, KV_TILE, stride=N_HEADS), :] = v[:, sl]


def _shared_kv(h, g, wk, wv, kg, tiles_per_seq, kept_tiles):
    t = h.shape[0]
    n_tiles = t // KV_TILE
    skipped = tiles_per_seq - kept_tiles
    row = lambda i: (i, 0)
    blk = pl.BlockSpec((KV_TILE, D_MODEL), row)

    def kept_block(i):
        return ((i // tiles_per_seq) * kept_tiles + jnp.maximum(i % tiles_per_seq - skipped, 0), 0)

    blk4 = pl.BlockSpec((KV_TILE * N_HEADS, HEAD_DIM), kept_block)
    kept = (n_tiles // tiles_per_seq) * kept_tiles * KV_TILE * N_HEADS
    return pl.pallas_call(
        functools.partial(_kv_kernel, tiles_per_seq=tiles_per_seq, kept_tiles=kept_tiles),
        grid=(n_tiles,),
        in_specs=[blk, _resident((1, D_MODEL)), _resident((D_MODEL, D_MODEL)),
                  _resident((D_MODEL, D_MODEL)), _resident((1, HEAD_DIM))],
        out_specs=[blk, blk, blk4, blk4],
        out_shape=[jax.ShapeDtypeStruct((t, D_MODEL), BF16),
                   jax.ShapeDtypeStruct((t, D_MODEL), BF16),
                   jax.ShapeDtypeStruct((kept, HEAD_DIM), F32),
                   jax.ShapeDtypeStruct((kept, HEAD_DIM), F32)],
        compiler_params=_params(("arbitrary",)),
        name="shared_kv",
    )(h, g.reshape(1, D_MODEL), wk, wv, kg.reshape(1, HEAD_DIM))


def _q_kernel(h_ref, g_ref, wq_ref, qg_ref, q_ref):
    hn = _rms(h_ref[...], g_ref[...]).astype(BF16)
    q = _head_rms(jnp.dot(hn, wq_ref[...], preferred_element_type=F32), qg_ref[...])
    q_ref[...] = (q * (HEAD_DIM ** -0.5)).astype(BF16)


def _q_proj(h, g, wq, qg):
    t = h.shape[0]
    row = lambda i: (i, 0)
    blk = pl.BlockSpec((TOKEN_TILE, D_MODEL), row)
    return pl.pallas_call(
        _q_kernel,
        grid=(t // TOKEN_TILE,),
        in_specs=[blk, _resident((1, D_MODEL)), _resident((D_MODEL, D_MODEL)),
                  _resident((1, HEAD_DIM))],
        out_specs=blk,
        out_shape=jax.ShapeDtypeStruct((t, D_MODEL), BF16),
        compiler_params=_params(("parallel",)),
        name="q_proj",
    )(h, g.reshape(1, D_MODEL), wq, qg.reshape(1, HEAD_DIM))


def _oproj_kernel(o_ref, h_ref, wo_ref, out_ref):
    out_ref[...] = h_ref[...] + jnp.dot(o_ref[...], wo_ref[...], preferred_element_type=F32)


def _o_proj(o, h, wo):
    t = h.shape[0]
    row = lambda i: (i, 0)
    blk = pl.BlockSpec((TOKEN_TILE, D_MODEL), row)
    return pl.pallas_call(
        _oproj_kernel,
        grid=(t // TOKEN_TILE,),
        in_specs=[blk, blk, _resident((D_MODEL, D_MODEL))],
        out_specs=blk,
        out_shape=jax.ShapeDtypeStruct((t, D_MODEL), F32),
        compiler_params=_params(("parallel",)),
        name="o_proj",
    )(o, h, wo)


def _attend_chunk(q, kp, kc, vp, vc, bias_p_ref, bias_c_ref):
    nt = (((1,), (1,)), ((), ()))

    def scores(h):
        sl = slice(h * HEAD_DIM, (h + 1) * HEAD_DIM)
        return (lax.dot_general(q[:, sl], kp[:, sl], nt, preferred_element_type=F32),
                lax.dot_general(q[:, sl], kc[:, sl], nt, preferred_element_type=F32))

    outs = []
    ahead = scores(0)
    for h in range(N_HEADS):
        sl = slice(h * HEAD_DIM, (h + 1) * HEAD_DIM)
        s_p, s_c = ahead
        if h + 1 < N_HEADS:
            ahead = scores(h + 1)
        s_p = s_p + bias_p_ref[h]
        s_c = s_c + bias_c_ref[h]
        m = jnp.maximum(jnp.max(s_p, axis=-1, keepdims=True),
                        jnp.max(s_c, axis=-1, keepdims=True))
        e_p = jnp.exp(s_p - m)
        e_c = jnp.exp(s_c - m)
        denom = jnp.sum(e_p, axis=-1, keepdims=True) + jnp.sum(e_c, axis=-1, keepdims=True)
        o = (jnp.dot(e_p.astype(BF16), vp[:, sl], preferred_element_type=F32)
             + jnp.dot(e_c.astype(BF16), vc[:, sl], preferred_element_type=F32))
        outs.append(o / denom)
    return jnp.concatenate(outs, axis=-1)


def _attn_prompt_kernel(q_ref, k0_ref, k1_ref, k2_ref, v0_ref, v1_ref, v2_ref, bias_ref, o_ref,
                        s_ref, e_ref, l_ref):
    nt = (((1,), (1,)), ((), ()))

    def head(h):
        return slice(h * HEAD_DIM, (h + 1) * HEAD_DIM)

    def scores(h):
        kh = jnp.concatenate([k0_ref[:, head(h)], k1_ref[:, head(h)], k2_ref[:, head(h)]], axis=0)
        s_ref[h % 2] = lax.dot_general(q_ref[:, head(h)], kh, nt, preferred_element_type=F32)

    def softmax(h):
        for c in range(ATTN_Q // ATTN_ROWS):
            rows = slice(c * ATTN_ROWS, (c + 1) * ATTN_ROWS)
            s = s_ref[h % 2, rows, :] + bias_ref[h, rows, :]
            e = jnp.exp(s - jnp.max(s, axis=-1, keepdims=True))
            l_ref[h % 2, rows, :] = jnp.broadcast_to(jnp.sum(e, axis=-1, keepdims=True),
                                                     (ATTN_ROWS, HEAD_DIM))
            e_ref[h % 2, rows, :] = e.astype(BF16)

    def values(h):
        vh = jnp.concatenate([v0_ref[:, head(h)], v1_ref[:, head(h)], v2_ref[:, head(h)]], axis=0)
        o = jnp.dot(e_ref[h % 2], vh, preferred_element_type=F32) / l_ref[h % 2]
        o_ref[:, head(h)] = o.astype(BF16)

    scores(0)
    for h in range(N_HEADS):
        if h + 1 < N_HEADS:
            scores(h + 1)
        softmax(h)
        values(h)


def _attn_prompt(q, k, v, bias):
    b, t, _ = q.shape
    n_past = BAND_PAST // ATTN_Q

    def kv_spec(m):
        return pl.BlockSpec((None, ATTN_Q, D_MODEL),
                            lambda i, g: (i, jnp.maximum(g - n_past + m, 0), 0))

    kv_specs = [kv_spec(m) for m in range(ATTN_KV // ATTN_Q)]
    return pl.pallas_call(
        _attn_prompt_kernel,
        grid=(b, t // ATTN_Q),
        in_specs=[pl.BlockSpec((None, ATTN_Q, D_MODEL), lambda i, g: (i, g, 0)),
                  *kv_specs, *kv_specs,
                  pl.BlockSpec((None, N_HEADS, ATTN_Q, ATTN_KV),
                               lambda i, g: (jnp.minimum(g, n_past), 0, 0, 0))],
        out_specs=pl.BlockSpec((None, ATTN_Q, D_MODEL), lambda i, g: (i, g, 0)),
        out_shape=jax.ShapeDtypeStruct((b, t, D_MODEL), BF16),
        scratch_shapes=[pltpu.VMEM((2, ATTN_Q, ATTN_KV), F32),
                        pltpu.VMEM((2, ATTN_Q, ATTN_KV), BF16),
                        pltpu.VMEM((2, ATTN_Q, HEAD_DIM), F32)],
        compiler_params=_params(("parallel", "arbitrary")),
        name="attn_prompt",
    )(q, k, k, k, v, v, v, bias)


def _attn_sample_kernel(q_ref, kp_ref, vp_ref, kc_ref, vc_ref, bp_ref, bc_ref, o_ref, kp_s, vp_s):
    for h in range(N_HEADS):
        sl = slice(h * HEAD_DIM, (h + 1) * HEAD_DIM)
        kp_s[:, sl] = kp_ref[pl.ds(h, BAND_PAST, stride=N_HEADS), :].astype(BF16)
        vp_s[:, sl] = vp_ref[pl.ds(h, BAND_PAST, stride=N_HEADS), :].astype(BF16)
    o_ref[...] = _attend_chunk(q_ref[...], kp_s[...], kc_ref[...], vp_s[...], vc_ref[...],
                               bp_ref, bc_ref).astype(BF16)


def _attn_sample(q, k_cache, v_cache, k_new, v_new, bias_p, bias_c):
    b = q.shape[0]
    new = pl.BlockSpec((None, CHUNK, D_MODEL), lambda i: (i, 0, 0))
    past = pl.BlockSpec((None, BAND_PAST * N_HEADS, HEAD_DIM), lambda i: (i, 0, 0))
    return pl.pallas_call(
        _attn_sample_kernel,
        grid=(b,),
        in_specs=[new, past, past, new, new,
                  _resident((N_HEADS, CHUNK, BAND_PAST)),
                  _resident((N_HEADS, CHUNK, CHUNK))],
        out_specs=new,
        out_shape=jax.ShapeDtypeStruct((b, CHUNK, D_MODEL), BF16),
        scratch_shapes=[pltpu.VMEM((BAND_PAST, D_MODEL), BF16),
                        pltpu.VMEM((BAND_PAST, D_MODEL), BF16)],
        compiler_params=_params(("parallel",)),
        name="attn_sample",
    )(q, k_cache, v_cache, k_new, v_new, bias_p, bias_c)


def _ssm_kernel(x_ref, g_ref, d_ref, wb_ref, wc_ref, are_ref, aim_ref, h0_ref,
                z_ref, hT_ref, lhs_ref, buf_ref, st_ref, sc_ref, *, n_seq, n_t):
    ti = pl.program_id(1)
    tt = SSM_TIME_TILE
    n_half = SUBLANES // n_seq
    n_st_blocks = SSM_GB_ST // LANES

    @pl.when(ti == 0)
    def _():
        st_ref[...] = h0_ref[...]

    for j in range(n_seq):
        x = x_ref[j]
        scale = lax.rsqrt(jnp.mean(x * x, axis=-1, keepdims=True) + RMS_EPS)
        sc_ref[j] = jnp.broadcast_to(scale, (tt, LANES))
        lhs_ref[j * tt:(j + 1) * tt, :] = (x * scale * g_ref[...]).astype(BF16)

    def project_in(p):
        slot = p % SSM_BUFFERS
        for half in range(n_half):
            gb = p * n_half + half
            ch = slice(gb * SSM_GB_CH, (gb + 1) * SSM_GB_CH)
            bu = jnp.dot(lhs_ref[:, ch], wb_ref[gb], preferred_element_type=F32)
            for j in range(n_seq):
                for k in range(2 * n_st_blocks):
                    buf_ref[slot, k, pl.ds(half * n_seq + j, tt, stride=SUBLANES), :] = (
                        bu[j * tt:(j + 1) * tt, k * LANES:(k + 1) * LANES])

    def recur(p):
        slot = p % SSM_BUFFERS
        a_re = are_ref[p]
        a_im = aim_ref[p]
        h_re = st_ref[p, :, :SSM_GB_ST]
        h_im = st_ref[p, :, SSM_GB_ST:]
        for t in range(tt):
            r = slice(t * SUBLANES, (t + 1) * SUBLANES)
            v_re = jnp.concatenate([buf_ref[slot, k, r, :] for k in range(n_st_blocks)], axis=-1)
            v_im = jnp.concatenate([buf_ref[slot, n_st_blocks + k, r, :]
                                    for k in range(n_st_blocks)], axis=-1)
            n_re = a_re * h_re - a_im * h_im + v_re
            n_im = a_re * h_im + a_im * h_re + v_im
            for k in range(n_st_blocks):
                lanes = slice(k * LANES, (k + 1) * LANES)
                buf_ref[slot, k, r, :] = n_re[:, lanes]
                buf_ref[slot, n_st_blocks + k, r, :] = n_im[:, lanes]
            h_re, h_im = n_re, n_im
        st_ref[p, :, :SSM_GB_ST] = h_re
        st_ref[p, :, SSM_GB_ST:] = h_im

    def project_out(p):
        slot = p % SSM_BUFFERS
        for half in range(n_half):
            gb = p * n_half + half
            hs = jnp.concatenate(
                [jnp.concatenate(
                    [buf_ref[slot, k, pl.ds(half * n_seq + j, tt, stride=SUBLANES), :]
                     for k in range(2 * n_st_blocks)], axis=-1) for j in range(n_seq)], axis=0)
            y = jnp.dot(hs.astype(BF16), wc_ref[gb], preferred_element_type=F32)
            for kk in range(SSM_GB_CH // LANES):
                lanes = slice(gb * SSM_GB_CH + kk * LANES, gb * SSM_GB_CH + (kk + 1) * LANES)
                for j in range(n_seq):
                    u = x_ref[j, :, lanes] * sc_ref[j] * g_ref[:, lanes]
                    yj = y[j * tt:(j + 1) * tt, kk * LANES:(kk + 1) * LANES]
                    z_ref[j, :, lanes] = jax.nn.gelu(yj + d_ref[:, lanes] * u).astype(BF16)

    n_pass = SSM_GB // n_half
    project_in(0)
    for p in range(n_pass):
        if p + 1 < n_pass:
            project_in(p + 1)
        recur(p)
        project_out(p)

    @pl.when(ti == n_t - 1)
    def _():
        hT_ref[...] = st_ref[...]


def _ssm(x, g, d, wb, wc, a_re, a_im, h0, n_seq):
    b, t, _ = x.shape
    n_grp = b // n_seq
    n_t = t // SSM_TIME_TILE
    rows = SSM_TIME_TILE * SUBLANES
    n_half = SUBLANES // n_seq
    n_pass = SSM_GB // n_half
    st_block = (None, n_pass, SUBLANES, 2 * SSM_GB_ST)

    def per_row(a):
        a = jnp.broadcast_to(a.reshape(n_pass, n_half, 1, SSM_GB_ST), (n_pass, n_half, n_seq, SSM_GB_ST))
        return a.reshape(n_pass, SUBLANES, SSM_GB_ST)

    a_re, a_im = per_row(a_re), per_row(a_im)
    return pl.pallas_call(
        functools.partial(_ssm_kernel, n_seq=n_seq, n_t=n_t),
        grid=(n_grp, n_t),
        in_specs=[pl.BlockSpec((n_seq, SSM_TIME_TILE, D_MODEL), lambda s, i: (s, i, 0)),
                  _resident((1, D_MODEL)),
                  _resident((1, D_MODEL)),
                  _resident((SSM_GB, SSM_GB_CH, 2 * SSM_GB_ST)),
                  _resident((SSM_GB, 2 * SSM_GB_ST, SSM_GB_CH)),
                  _resident((n_pass, SUBLANES, SSM_GB_ST)),
                  _resident((n_pass, SUBLANES, SSM_GB_ST)),
                  pl.BlockSpec(st_block, lambda s, i: (s, 0, 0, 0))],
        out_specs=[pl.BlockSpec((n_seq, SSM_TIME_TILE, D_MODEL), lambda s, i: (s, i, 0)),
                   pl.BlockSpec(st_block, lambda s, i: (s, 0, 0, 0))],
        out_shape=[jax.ShapeDtypeStruct((b, t, D_MODEL), BF16),
                   jax.ShapeDtypeStruct((n_grp, n_pass, SUBLANES, 2 * SSM_GB_ST), F32)],
        scratch_shapes=[pltpu.VMEM((n_seq * SSM_TIME_TILE, D_MODEL), BF16),
                        pltpu.VMEM((SSM_BUFFERS, 2 * SSM_GB_ST // LANES, rows, LANES), F32),
                        pltpu.VMEM((n_pass, SUBLANES, 2 * SSM_GB_ST), F32),
                        pltpu.VMEM((n_seq, SSM_TIME_TILE, LANES), F32)],
        compiler_params=_params(("parallel", "arbitrary")),
        name="ssm",
    )(x, g.reshape(1, D_MODEL), d.reshape(1, D_MODEL), wb, wc, a_re, a_im, h0)


def _ssm_weights(lam_re, lam_im, log_dt, b_re, b_im, c_re, c_im):
    lam_re, lam_im = lam_re.astype(F32), lam_im.astype(F32)
    dt = jnp.exp(log_dt.astype(F32))[:, None]
    mag = jnp.exp(lam_re * dt)
    bar_re = mag * jnp.cos(lam_im * dt)
    bar_im = mag * jnp.sin(lam_im * dt)
    norm = lam_re * lam_re + lam_im * lam_im
    q_re = (((bar_re - 1.0) * lam_re + bar_im * lam_im) / norm)[:, :, None]
    q_im = ((bar_im * lam_re - (bar_re - 1.0) * lam_im) / norm)[:, :, None]
    b_re, b_im = b_re.astype(F32), b_im.astype(F32)
    bb_re = q_re * b_re - q_im * b_im
    bb_im = q_re * b_im + q_im * b_re
    gl = SSM_GROUPS // SSM_GB
    eye = jnp.eye(gl, dtype=F32)

    def in_proj(w):
        w = w.transpose(0, 2, 1).reshape(SSM_GB, gl, SSM_GROUP, SSM_STATE)
        return jnp.einsum('bgcp,gh->bgchp', w, eye).reshape(SSM_GB, SSM_GB_CH, SSM_GB_ST)

    def out_proj(w):
        w = w.transpose(0, 2, 1).reshape(SSM_GB, gl, SSM_STATE, SSM_GROUP)
        return jnp.einsum('bgpc,gh->bgphc', w, eye).reshape(SSM_GB, SSM_GB_ST, SSM_GB_CH)

    wb = jnp.concatenate([in_proj(bb_re), in_proj(bb_im)], axis=-1)
    wc = jnp.concatenate([out_proj(c_re.astype(F32)), out_proj(-c_im.astype(F32))], axis=1)
    a_re = bar_re.reshape(SSM_GB, SSM_GB_ST)
    a_im = bar_im.reshape(SSM_GB, SSM_GB_ST)
    return wb.astype(BF16), wc.astype(BF16), a_re, a_im


def _state_to_blocks(h_re, h_im, n_seq):
    b = h_re.shape[0]
    n_half = SUBLANES // n_seq
    n_pass = SSM_GB // n_half
    def one(h):
        h = h.reshape(b // n_seq, n_seq, n_pass, n_half, SSM_GB_ST).transpose(0, 2, 3, 1, 4)
        return h.reshape(b // n_seq, n_pass, SUBLANES, SSM_GB_ST)
    return jnp.concatenate([one(h_re), one(h_im)], axis=-1)


def _blocks_to_state(st, n_seq):
    n_grp, n_pass = st.shape[:2]
    n_half = SUBLANES // n_seq
    def one(h):
        h = h.reshape(n_grp, n_pass, n_half, n_seq, SSM_GB_ST).transpose(0, 3, 1, 2, 4)
        return h.reshape(1, n_grp * n_seq, SSM_GROUPS, SSM_STATE)
    return one(st[..., :SSM_GB_ST]), one(st[..., SSM_GB_ST:])


def _rel_bias(tab):
    tab = tab.astype(F32)
    band = BAND_PAST + CHUNK
    n_far = band - REL_CLIP
    far = jnp.broadcast_to(tab[:, 2 * REL_CLIP:], (N_HEADS, n_far))
    near = tab[:, 2 * REL_CLIP - n_far:2 * REL_CLIP][:, ::-1]
    prof = jnp.concatenate([far, near], axis=1)
    period = prof.shape[1]
    skew = jnp.tile(prof, (1, CHUNK))[:, :CHUNK * (period - 1)].reshape(N_HEADS, CHUNK, period - 1)
    bias = skew[:, :, CHUNK - 1:CHUNK - 1 + band]
    rows = [jnp.pad(bias, ((0, 0), (0, 0), (c * CHUNK, ATTN_KV - band - c * CHUNK)),
                    constant_values=MASK_VALUE) for c in range(ATTN_Q // CHUNK)]
    table = jnp.concatenate(rows, axis=1)
    key = jnp.arange(ATTN_KV)
    tables = [jnp.where(key >= BAND_PAST - v * ATTN_Q, table, MASK_VALUE)
              for v in range(BAND_PAST // ATTN_Q + 1)]
    return bias[:, :, :BAND_PAST], bias[:, :, BAND_PAST:], jnp.stack(tables)


def _trunk(x, p, ssm_h0, n_seq, cache, w):
    b, t, _ = x.shape
    n = b * t
    h = x.reshape(n, D_MODEL)

    h = _ffn(h, w['ffn1_norm_g'][0], *w['ffn1'][0])
    z, st = _ssm(h.reshape(b, t, D_MODEL), w['mix_norm_g'][0], w['ssm_d'][0],
                 w['ssm_wb'], w['ssm_wc'], w['ssm_a_re'], w['ssm_a_im'], ssm_h0, n_seq)
    h = _glu(z.reshape(n, D_MODEL), h, w['ssm_w_glu_a'], w['ssm_w_glu_b'])
    h = _ffn(h, w['ffn2_norm_g'][0], *w['ffn2'][0])
    h = _ple(h, p[0].reshape(n, PLE_DIM), w['ple_norm_g'][0], w['ple_w_gate'][0], w['ple_w_proj'][0])

    rows = min(BAND_PAST, t)
    if t >= KV_TILE:
        assert rows % KV_TILE == 0 and t % KV_TILE == 0
        tiles_per_seq, kept_tiles = t // KV_TILE, rows // KV_TILE
    else:
        assert rows == t and KV_TILE % t == 0
        tiles_per_seq, kept_tiles = 1, 1
    kb, vb, k4, v4 = _shared_kv(h, w['kv_norm_g'], w['w_k'], w['w_v'], w['k_norm_g'],
                                tiles_per_seq, kept_tiles)
    h = _ffn(h, w['ffn1_norm_g'][1], *w['ffn1'][1])
    q = _q_proj(h, w['mix_norm_g'][1], w['w_q'], w['q_norm_g']).reshape(b, t, D_MODEL)
    kb = kb.reshape(b, t, D_MODEL)
    vb = vb.reshape(b, t, D_MODEL)
    if cache is None:
        o = _attn_prompt(q, kb, vb, w['bias_band'])
    else:
        o = _attn_sample(q, cache[0], cache[1], kb, vb, w['bias_p'], w['bias_c'])
    h = _o_proj(o.reshape(n, D_MODEL), h, w['w_o'])
    h = _ffn(h, w['ffn2_norm_g'][1], *w['ffn2'][1])
    h = _ple(h, p[1].reshape(n, PLE_DIM), w['ple_norm_g'][1], w['ple_w_gate'][1], w['ple_w_proj'][1])
    return (h.reshape(b, t, D_MODEL), st,
            k4.reshape(b, rows, N_HEADS, HEAD_DIM), v4.reshape(b, rows, N_HEADS, HEAD_DIM))


def kernel(x_prompt, x_sample, state_ssm_re, state_ssm_im, cache_k, cache_v, p_prompt, p_sample, ffn1_norm_g, ffn1_w_gate, ffn1_w_up, ffn1_w_down, ffn2_norm_g, ffn2_w_gate, ffn2_w_up, ffn2_w_down, mix_norm_g, ssm_lam_re, ssm_lam_im, ssm_log_dt, ssm_b_re, ssm_b_im, ssm_c_re, ssm_c_im, ssm_d, ssm_w_glu_a, ssm_w_glu_b, kv_norm_g, w_k, w_v, k_norm_g, w_q, q_norm_g, rel_bias, w_o, ple_norm_g, ple_w_gate, ple_w_proj):
    bf = _to_bf16
    wb, wc, a_re, a_im = _ssm_weights(ssm_lam_re[0], ssm_lam_im[0], ssm_log_dt[0],
                                      ssm_b_re[0], ssm_b_im[0], ssm_c_re[0], ssm_c_im[0])
    bias_p, bias_c, bias_band = _rel_bias(rel_bias[0])
    w = dict(
        ffn1_norm_g=ffn1_norm_g, ffn2_norm_g=ffn2_norm_g,
        ffn1=[(bf(ffn1_w_gate, i), bf(ffn1_w_up, i), bf(ffn1_w_down, i)) for i in range(2)],
        ffn2=[(bf(ffn2_w_gate, i), bf(ffn2_w_up, i), bf(ffn2_w_down, i)) for i in range(2)],
        mix_norm_g=mix_norm_g, ssm_d=ssm_d,
        ssm_wb=wb, ssm_wc=wc, ssm_a_re=a_re, ssm_a_im=a_im,
        ssm_w_glu_a=bf(ssm_w_glu_a, 0), ssm_w_glu_b=bf(ssm_w_glu_b, 0),
        kv_norm_g=kv_norm_g, w_k=bf(w_k), w_v=bf(w_v), k_norm_g=k_norm_g,
        w_q=bf(w_q, 0), q_norm_g=q_norm_g[0], bias_p=bias_p, bias_c=bias_c,
        bias_band=bias_band, w_o=bf(w_o, 0),
        ple_norm_g=ple_norm_g, ple_w_gate=[bf(ple_w_gate, i) for i in range(2)],
        ple_w_proj=[bf(ple_w_proj, i) for i in range(2)],
    )

    bp = x_prompt.shape[0]
    bs = x_sample.shape[0]
    n_seq_p = bp
    n_seq_s = SUBLANES
    zeros = jnp.zeros((bp, SSM_GROUPS, SSM_STATE), F32)
    h0_p = _state_to_blocks(zeros, zeros, n_seq_p)
    h0_s = _state_to_blocks(state_ssm_re[0].astype(F32), state_ssm_im[0].astype(F32), n_seq_s)

    y_p, st_p, k_p, v_p = _trunk(x_prompt, p_prompt, h0_p, n_seq_p, None, w)
    cache = (cache_k.reshape(bs, BAND_PAST * N_HEADS, HEAD_DIM),
             cache_v.reshape(bs, BAND_PAST * N_HEADS, HEAD_DIM))
    y_s, st_s, k_s, v_s = _trunk(x_sample, p_sample, h0_s, n_seq_s, cache, w)

    sp_re, sp_im = _blocks_to_state(st_p, n_seq_p)
    ss_re, ss_im = _blocks_to_state(st_s, n_seq_s)
    return (y_p, y_s, sp_re, sp_im, k_p, v_p, ss_re, ss_im, k_s, v_s)
```
